```python
import math
import jax, jax.numpy as jnp
from jax import lax
import numpy as np

D_MODEL = 4096
BATCH = 2
SEQ = 8192
DEPTH = 2

N_A_LAYERS = DEPTH // 2
N_B_LAYERS = DEPTH - N_A_LAYERS
CONV_KERNEL = 31
HEAD_DIM = 128
N_DIFF_HEADS = D_MODEL // (2 * HEAD_DIM)
Q_BLOCK = 128
N_EXPERTS = 64
N_GROUPS = 8
TOPK_GROUPS = 4
TOP_K = 8
EXPERT_FF = 256
SHARED_FF = 256
ROUTED_SCALE = 2.5
MOE_TOKEN_BLOCK = 1024
DEEPNORM_ALPHA = (2 * DEPTH) ** 0.25
DEEPNORM_BETA = (8 * DEPTH) ** -0.25
LN_EPS = 1e-5
ADA_STD = 0.25

kernel_name = 'yoco_conformer_diffattn_moe_trunk'


def layer_norm(x, g, b):
    xf = x.astype(jnp.float32)
    mu = jnp.mean(xf, axis=-1, keepdims=True)
    var = jnp.mean(jnp.square(xf - mu), axis=-1, keepdims=True)
    return ((xf - mu) * lax.rsqrt(var + LN_EPS)).astype(x.dtype) * g + b


def rms_norm(x, g):
    xf = x.astype(jnp.float32)
    return (xf * lax.rsqrt(jnp.mean(xf * xf, axis=-1, keepdims=True) + LN_EPS)).astype(x.dtype) * g


def conformer_conv(u, pw1_w, pw1_b, dw_w, dw_b, ln_g, ln_b, pw2_w, pw2_b):
    h = u @ pw1_w + pw1_b
    a, g = jnp.split(h, 2, axis=-1)
    h = a * jax.nn.sigmoid(g)
    h = lax.conv_general_dilated(
        h, dw_w[:, None, :], window_strides=(1,),
        padding=[(CONV_KERNEL - 1, 0)],
        dimension_numbers=('NWC', 'WIO', 'NWC'),
        feature_group_count=D_MODEL) + dw_b
    h = jax.nn.silu(layer_norm(h, ln_g, ln_b))
    return h @ pw2_w + pw2_b


def alibi_slopes(n):
    return 2.0 ** (-8.0 * jnp.arange(1, n + 1, dtype=jnp.float32) / n)


def diff_attention(u, k, v, wq, lam_p, subln_g, wo, lam_init):
    B, S, _ = u.shape
    H, d = N_DIFF_HEADS, HEAD_DIM
    q = (u @ wq).reshape(B, S, 2 * H, d) * (d ** -0.5)
    nb = S // Q_BLOCK
    q_blocks = q.reshape(B, nb, Q_BLOCK, 2 * H, d).transpose(1, 0, 3, 2, 4)
    lp = lam_p.astype(jnp.float32)
    lam = jnp.exp(jnp.sum(lp[0] * lp[1])) - jnp.exp(jnp.sum(lp[2] * lp[3])) + lam_init
    slopes = alibi_slopes(H)
    kpos = jnp.arange(S, dtype=jnp.int32)

    def block(args):
        qb, start = args
        s = jnp.einsum('bhqd,bhkd->bhqk', qb, k).astype(jnp.float32)
        s = s.reshape(B, H, 2, Q_BLOCK, S)
        dist = (start + jnp.arange(Q_BLOCK, dtype=jnp.int32))[:, None] - kpos[None, :]
        bias = -slopes[:, None, None] * dist.astype(jnp.float32)
        s = jnp.where((dist >= 0)[None, None, None], s + bias[None, :, None], -jnp.inf)
        p = jax.nn.softmax(s, axis=-1)
        w = p[:, :, 0] - lam * p[:, :, 1]
        return jnp.einsum('bhqk,bhkv->bhqv', w.astype(v.dtype), v)

    starts = jnp.arange(nb, dtype=jnp.int32) * Q_BLOCK
    o = lax.map(block, (q_blocks, starts))
    o = o.transpose(1, 0, 3, 2, 4).reshape(B, S, H, 2 * d)
    o = rms_norm(o, subln_g) * (1.0 - lam_init)
    return o.reshape(B, S, H * 2 * d) @ wo


def moe(u, router_w, router_bias, w_gate, w_up, w_down, s_gate, s_up, s_down):
    B, S, D = u.shape
    n_tok = B * S
    blk = math.gcd(n_tok, MOE_TOKEN_BLOCK)
    xt = u.reshape(n_tok // blk, blk, D)
    rows = jnp.arange(blk)[:, None]

    def block(xb):
        scores = jax.nn.sigmoid((xb @ router_w).astype(jnp.float32))
        sel = scores + router_bias.astype(jnp.float32)
        grp = sel.reshape(blk, N_GROUPS, N_EXPERTS // N_GROUPS)
        grp_score = jnp.sum(lax.top_k(grp, 2)[0], axis=-1)
        _, gidx = lax.top_k(grp_score, TOPK_GROUPS)
        gmask = jnp.zeros((blk, N_GROUPS), dtype=bool).at[rows, gidx].set(True)
        emask = jnp.repeat(gmask, N_EXPERTS // N_GROUPS, axis=1)
        _, eidx = lax.top_k(jnp.where(emask, sel, -jnp.inf), TOP_K)
        w = jnp.take_along_axis(scores, eidx, axis=1)
        w = w / jnp.sum(w, axis=-1, keepdims=True) * ROUTED_SCALE
        gates = jnp.zeros((blk, N_EXPERTS), jnp.float32).at[rows, eidx].set(w)
        h = jax.nn.silu(jnp.einsum('td,edf->tef', xb, w_gate)) * jnp.einsum('td,edf->tef', xb, w_up)
        y = jnp.einsum('tef,efd->td', h * gates.astype(h.dtype)[:, :, None], w_down)
        y_shared = (jax.nn.silu(xb @ s_gate) * (xb @ s_up)) @ s_down
        return y + y_shared

    return lax.map(block, xt).reshape(B, S, D)


def setup_inputs(seed: int = 0) -> dict:
    key = jax.random.key(seed)
    ks = iter(jax.random.split(key, 32))
    D = D_MODEL
    sd = D ** -0.5

    def nrm(shape, std):
        return jax.random.normal(next(ks), shape, jnp.float32) * std

    return {
        'x': nrm((BATCH, SEQ, D), 1.0),
        'c': nrm((BATCH, D), 1.0),
        'ada_w': nrm((DEPTH, D, 6 * D), ADA_STD * sd),
        'ada_b': nrm((DEPTH, 6 * D), 0.01),
        'ln_g': 1.0 + nrm((DEPTH, 2, D), 0.01),
        'ln_b': nrm((DEPTH, 2, D), 0.01),
        'conv_pw1_w': nrm((N_A_LAYERS, D, 2 * D), sd),
        'conv_pw1_b': nrm((N_A_LAYERS, 2 * D), 0.01),
        'conv_dw_w': nrm((N_A_LAYERS, CONV_KERNEL, D), CONV_KERNEL ** -0.5),
        'conv_dw_b': nrm((N_A_LAYERS, D), 0.01),
        'conv_ln_g': 1.0 + nrm((N_A_LAYERS, D), 0.01),
        'conv_ln_b': nrm((N_A_LAYERS, D), 0.01),
        'conv_pw2_w': nrm((N_A_LAYERS, D, D), sd * DEEPNORM_BETA),
        'conv_pw2_b': nrm((N_A_LAYERS, D), 0.01),
        'attn_wk': nrm((D, D), sd),
        'attn_wv': nrm((D, D), sd),
        'attn_wq': nrm((N_B_LAYERS, D, D), sd),
        'attn_lambda': nrm((N_B_LAYERS, 4, HEAD_DIM), 0.1),
        'attn_subln_g': 1.0 + nrm((N_B_LAYERS, 2 * HEAD_DIM), 0.01),
        'attn_wo': nrm((N_B_LAYERS, D, D), sd * DEEPNORM_BETA),
        'router_w': nrm((DEPTH, D, N_EXPERTS), sd),
        'router_bias': nrm((DEPTH, N_EXPERTS), 0.01),
        'exp_w_gate': nrm((DEPTH, N_EXPERTS, D, EXPERT_FF), sd),
        'exp_w_up': nrm((DEPTH, N_EXPERTS, D, EXPERT_FF), sd),
        'exp_w_down': nrm((DEPTH, N_EXPERTS, EXPERT_FF, D), EXPERT_FF ** -0.5 * DEEPNORM_BETA),
        'shared_w_gate': nrm((DEPTH, D, SHARED_FF), sd),
        'shared_w_up': nrm((DEPTH, D, SHARED_FF), sd),
        'shared_w_down': nrm((DEPTH, SHARED_FF, D), SHARED_FF ** -0.5 * DEEPNORM_BETA),
    }


def reference(x, c, ada_w, ada_b, ln_g, ln_b, conv_pw1_w, conv_pw1_b, conv_dw_w, conv_dw_b,
              conv_ln_g, conv_ln_b, conv_pw2_w, conv_pw2_b, attn_wk, attn_wv, attn_wq,
              attn_lambda, attn_subln_g, attn_wo, router_w, router_bias, exp_w_gate, exp_w_up,
              exp_w_down, shared_w_gate, shared_w_up, shared_w_down):
    B, S, _ = x.shape
    H, d = N_DIFF_HEADS, HEAD_DIM
    cond = jax.nn.silu(c)
    k_sh = None
    v_sh = None
    for l in range(DEPTH):
        mod = cond @ ada_w[l] + ada_b[l]
        sh_m, sc_m, g_m, sh_f, sc_f, g_f = [m[:, None, :] for m in jnp.split(mod, 6, axis=-1)]
        u = x * (1.0 + sc_m) + sh_m
        if l < N_A_LAYERS:
            y = conformer_conv(u, conv_pw1_w[l], conv_pw1_b[l], conv_dw_w[l], conv_dw_b[l],
                               conv_ln_g[l], conv_ln_b[l], conv_pw2_w[l], conv_pw2_b[l])
        else:
            if l == N_A_LAYERS:
                k_sh = (x @ attn_wk).reshape(B, S, 2 * H, d).transpose(0, 2, 1, 3)
                v_sh = (x @ attn_wv).reshape(B, S, H, 2 * d).transpose(0, 2, 1, 3)
            j = l - N_A_LAYERS
            lam_init = 0.8 - 0.6 * math.exp(-0.3 * l)
            y = diff_attention(u, k_sh, v_sh, attn_wq[j], attn_lambda[j], attn_subln_g[j],
                               attn_wo[j], lam_init)
        x = layer_norm(DEEPNORM_ALPHA * x + (1.0 + g_m) * y, ln_g[l, 0], ln_b[l, 0])
        u = x * (1.0 + sc_f) + sh_f
        y = moe(u, router_w[l], router_bias[l], exp_w_gate[l], exp_w_up[l], exp_w_down[l],
                shared_w_gate[l], shared_w_up[l], shared_w_down[l])
        x = layer_norm(DEEPNORM_ALPHA * x + (1.0 + g_f) * y, ln_g[l, 1], ln_b[l, 1])
    return x
```

```python
import functools
import math

import jax
import jax.numpy as jnp
import numpy as np
from jax import lax
from jax.experimental import pallas as pl
from jax.experimental.pallas import tpu as pltpu

F32 = jnp.float32
BF16 = jnp.bfloat16
U32 = jnp.uint32
I32 = jnp.int32

HEAD_DIM = 128
N_GROUPS = 8
TOPK_GROUPS = 4
TOP_K = 8
ROUTED_SCALE = 2.5
LN_EPS = 1e-5

LANES = 128
ATTN_TILE = 1024
EXPERT_TILE = 256
ROUTER_TILE = 512
COMBINE_TILE = 128
VMEM_LIMIT = 56 * 1024 * 1024


def _cp(*sem):
    return pltpu.CompilerParams(dimension_semantics=sem, vmem_limit_bytes=VMEM_LIMIT)


def _pick(n, pref):
    t = min(n, pref)
    while n % t:
        t //= 2
    return t


def _bf16_bits(v):
    return lax.bitcast_convert_type(v.astype(BF16).astype(F32), U32)


def _store_packed(ref, val, rows, r_words):
    half = r_words * LANES
    for c in range(r_words):
        lo = _bf16_bits(val[:, c * LANES:(c + 1) * LANES])
        hi = _bf16_bits(val[:, half + c * LANES:half + (c + 1) * LANES])
        ref[pl.ds(c, rows, stride=r_words), :] = (lo >> 16) | hi


def _load_packed(ref, rows, r_words):
    lo, hi = [], []
    for c in range(r_words):
        w = ref[pl.ds(c, rows, stride=r_words), :]
        lo.append(lax.bitcast_convert_type(w << 16, F32))
        hi.append(lax.bitcast_convert_type(w & jnp.uint32(0xFFFF0000), F32))
    return lo, hi


def _ada_kernel(c_ref, w_ref, b_ref, o_ref, cs_ref, *, nb, tn):
    @pl.when((pl.program_id(0) == 0) & (pl.program_id(1) == 0))
    def _():
        c = c_ref[...]
        cs_ref[...] = c * jax.nn.sigmoid(c)

    for b in range(nb):
        parts = []
        for j in range(tn // LANES):
            prod = w_ref[:, j * LANES:(j + 1) * LANES] * cs_ref[b]
            parts.append(jnp.sum(prod, axis=0, keepdims=True))
        o_ref[b:b + 1, :] = jnp.concatenate(parts, axis=1) + b_ref[...]


def _ada_mod(c, ada_w, ada_b):
    nl, d, n6 = ada_w.shape
    nb = c.shape[0]
    tn = _pick(n6, 512)
    c_b = jnp.broadcast_to(c[:, :, None], (nb, d, LANES))
    return pl.pallas_call(
        functools.partial(_ada_kernel, nb=nb, tn=tn),
        out_shape=jax.ShapeDtypeStruct((nl, nb, n6), F32),
        grid=(nl, n6 // tn),
        in_specs=[
            pl.BlockSpec((nb, d, LANES), lambda l, j: (0, 0, 0)),
            pl.BlockSpec((None, d, tn), lambda l, j: (l, 0, j)),
            pl.BlockSpec((None, 1, tn), lambda l, j: (l, 0, j)),
        ],
        out_specs=pl.BlockSpec((None, nb, tn), lambda l, j: (l, 0, j)),
        scratch_shapes=[pltpu.VMEM((nb, d, LANES), F32)],
        compiler_params=_cp("arbitrary", "arbitrary"),
        name="ada_mod",
    )(c_b, ada_w, ada_b.reshape(nl, 1, n6))


def _layer_norm_rows(z, g, b):
    mu = jnp.mean(z, axis=-1, keepdims=True)
    zc = z - mu
    var = jnp.mean(zc * zc, axis=-1, keepdims=True)
    return zc * lax.rsqrt(var + LN_EPS) * g + b


def _mod_spec(d, tiles_per_batch, chunk):
    return pl.BlockSpec((None, 1, d), lambda i, _c=chunk, _t=tiles_per_batch: (i // _t, 0, _c))


def _modulate_kernel(x_ref, sc_ref, sh_ref, u_ref):
    u_ref[...] = (x_ref[...] * (1.0 + sc_ref[...]) + sh_ref[...]).astype(u_ref.dtype)


def _modulate(x2, mod_l, seq, sc_chunk, sh_chunk):
    n, d = x2.shape
    ts = _pick(seq, 512)
    tpb = seq // ts
    return pl.pallas_call(
        _modulate_kernel,
        out_shape=jax.ShapeDtypeStruct((n, d), BF16),
        grid=(n // ts,),
        in_specs=[pl.BlockSpec((ts, d), lambda i: (i, 0)),
                  _mod_spec(d, tpb, sc_chunk), _mod_spec(d, tpb, sh_chunk)],
        out_specs=pl.BlockSpec((ts, d), lambda i: (i, 0)),
        compiler_params=_cp("parallel"),
        name="modulate",
    )(x2, mod_l, mod_l)


def _mm_kernel(a_ref, w_ref, b_ref, o_ref, *, scale):
    acc = jnp.dot(a_ref[...], w_ref[...], preferred_element_type=F32) + b_ref[...]
    if scale != 1.0:
        acc = acc * scale
    o_ref[...] = acc.astype(o_ref.dtype)


def _matmul(a, w, bias, *, scale=1.0, out_dtype=BF16):
    m, k = a.shape
    n = w.shape[1]
    tm, tn = _pick(m, 1024), _pick(n, 1024)
    return pl.pallas_call(
        functools.partial(_mm_kernel, scale=scale),
        out_shape=jax.ShapeDtypeStruct((m, n), out_dtype),
        grid=(m // tm, n // tn),
        in_specs=[pl.BlockSpec((tm, k), lambda i, j: (i, 0)),
                  pl.BlockSpec((k, tn), lambda i, j: (0, j)),
                  pl.BlockSpec((1, tn), lambda i, j: (0, j))],
        out_specs=pl.BlockSpec((tm, tn), lambda i, j: (i, j)),
        compiler_params=_cp("parallel", "parallel"),
        name="matmul",
    )(a, w, bias.reshape(1, n))


def _glu_kernel(a_ref, wa_ref, wg_ref, ba_ref, bg_ref, o_ref):
    a = a_ref[...]
    lin = jnp.dot(a, wa_ref[...], preferred_element_type=F32) + ba_ref[...]
    gate = jnp.dot(a, wg_ref[...], preferred_element_type=F32) + bg_ref[...]
    o_ref[...] = (lin * jax.nn.sigmoid(gate)).astype(o_ref.dtype)


def _matmul_glu(a, w, bias):
    m, k = a.shape
    n = w.shape[1] // 2
    tm, tn = _pick(m, 1024), _pick(n, 512)
    nj = n // tn
    b2 = bias.reshape(1, 2 * n)
    return pl.pallas_call(
        _glu_kernel,
        out_shape=jax.ShapeDtypeStruct((m, n), BF16),
        grid=(m // tm, nj),
        in_specs=[pl.BlockSpec((tm, k), lambda i, j: (i, 0)),
                  pl.BlockSpec((k, tn), lambda i, j: (0, j)),
                  pl.BlockSpec((k, tn), lambda i, j, _nj=nj: (0, j + _nj)),
                  pl.BlockSpec((1, tn), lambda i, j: (0, j)),
                  pl.BlockSpec((1, tn), lambda i, j, _nj=nj: (0, j + _nj))],
        out_specs=pl.BlockSpec((tm, tn), lambda i, j: (i, j)),
        compiler_params=_cp("parallel", "parallel"),
        name="matmul_glu",
    )(a, w, w, b2, b2)


def _conv_kernel(cur_ref, halo_ref, w_ref, dwb_ref, g_ref, b_ref, o_ref, buf_ref, cv_ref,
                 *, ts, hb, kw, rc, cw):
    d = cur_ref.shape[-1]
    first = pl.program_id(1) == 0
    buf_ref[0:hb, :] = jnp.where(first, 0.0, halo_ref[...].astype(F32))
    buf_ref[hb:hb + ts, :] = cur_ref[...].astype(F32)
    for r in range(ts // rc):
        for c in range(d // cw):
            cols = slice(c * cw, (c + 1) * cw)
            acc = jnp.zeros((rc, cw), F32)
            for k in range(kw):
                off = hb - (kw - 1) + k + r * rc
                acc = acc + buf_ref[off:off + rc, cols] * w_ref[k:k + 1, cols]
            cv_ref[r * rc:(r + 1) * rc, cols] = acc + dwb_ref[:, cols]
    y = _layer_norm_rows(cv_ref[...], g_ref[...], b_ref[...])
    o_ref[...] = (y * jax.nn.sigmoid(y)).astype(o_ref.dtype)


def _conv_ln_swish(h, dw_w, dw_b, g, b, batch, seq):
    n, d = h.shape
    kw = dw_w.shape[0]
    hb = 32
    assert kw - 1 <= hb and seq % hb == 0
    ts = _pick(seq, 128)
    rc, cw = _pick(ts, 64), _pick(d, 256)
    h3 = h.reshape(batch, seq, d)
    bpt = ts // hb
    out = pl.pallas_call(
        functools.partial(_conv_kernel, ts=ts, hb=hb, kw=kw, rc=rc, cw=cw),
        out_shape=jax.ShapeDtypeStruct((batch, seq, d), BF16),
        grid=(batch, seq // ts),
        in_specs=[pl.BlockSpec((None, ts, d), lambda bi, i: (bi, i, 0)),
                  pl.BlockSpec((None, hb, d),
                               lambda bi, i, _b=bpt: (bi, jnp.maximum(i * _b - 1, 0), 0)),
                  pl.BlockSpec((kw, d), lambda bi, i: (0, 0)),
                  pl.BlockSpec((1, d), lambda bi, i: (0, 0)),
                  pl.BlockSpec((1, d), lambda bi, i: (0, 0)),
                  pl.BlockSpec((1, d), lambda bi, i: (0, 0))],
        out_specs=pl.BlockSpec((None, ts, d), lambda bi, i: (bi, i, 0)),
        scratch_shapes=[pltpu.VMEM((hb + ts, d), F32), pltpu.VMEM((ts, d), F32)],
        compiler_params=_cp("parallel", "parallel"),
        name="dwconv_ln_swish",
    )(h3, h3, dw_w, dw_b.reshape(1, d), g.reshape(1, d), b.reshape(1, d))
    return out.reshape(n, d)


def _emit_outputs(xn, sc_ref, sh_ref, outs, kinds, rows, r_words):
    u = None
    for ref, kind in zip(outs, kinds):
        if kind == "x":
            ref[...] = xn
        elif kind == "xb":
            ref[...] = xn.astype(BF16)
        else:
            if u is None:
                u = xn * (1.0 + sc_ref[...]) + sh_ref[...]
            if kind == "ub":
                ref[...] = u.astype(BF16)
            else:
                _store_packed(ref, u, rows, r_words)


def _post_kernel(x_ref, y_ref, gate_ref, lng_ref, lnb_ref, sc_ref, sh_ref, *outs,
                 alpha, kinds, rows, r_words):
    z = alpha * x_ref[...] + (1.0 + gate_ref[...]) * y_ref[...].astype(F32)
    xn = _layer_norm_rows(z, lng_ref[...], lnb_ref[...])
    _emit_outputs(xn, sc_ref, sh_ref, outs, kinds, rows, r_words)


def _out_shapes_specs(kinds, n, d, ts, r_words):
    shapes, specs = [], []
    for kind in kinds:
        if kind == "x":
            shapes.append(jax.ShapeDtypeStruct((n, d), F32))
            specs.append(pl.BlockSpec((ts, d), lambda i, *_: (i, 0)))
        elif kind in ("xb", "ub"):
            shapes.append(jax.ShapeDtypeStruct((n, d), BF16))
            specs.append(pl.BlockSpec((ts, d), lambda i, *_: (i, 0)))
        else:
            shapes.append(jax.ShapeDtypeStruct((n * r_words, LANES), U32))
            specs.append(pl.BlockSpec((ts * r_words, LANES), lambda i, *_: (i, 0)))
    return shapes, specs


def _post(x2, y, mod_l, gate_chunk, lng, lnb, next_mod, sc_chunk, sh_chunk, seq, alpha, kinds):
    n, d = x2.shape
    ts = _pick(seq, 256)
    tpb = seq // ts
    r_words = d // 2 // LANES
    shapes, specs = _out_shapes_specs(kinds, n, d, ts, r_words)
    return pl.pallas_call(
        functools.partial(_post_kernel, alpha=alpha, kinds=kinds, rows=ts, r_words=r_words),
        out_shape=shapes,
        grid=(n // ts,),
        in_specs=[pl.BlockSpec((ts, d), lambda i: (i, 0)),
                  pl.BlockSpec((ts, d), lambda i: (i, 0)),
                  _mod_spec(d, tpb, gate_chunk),
                  pl.BlockSpec((1, d), lambda i: (0, 0)),
                  pl.BlockSpec((1, d), lambda i: (0, 0)),
                  _mod_spec(d, tpb, sc_chunk), _mod_spec(d, tpb, sh_chunk)],
        out_specs=specs,
        compiler_params=_cp("parallel"),
        name="residual_ln",
    )(x2, y, mod_l, lng.reshape(1, d), lnb.reshape(1, d), next_mod, next_mod)


def _router_kernel(u_ref, rwt_ref, rb_ref, tri_ref, eid_ref, gate_ref, rk_ref, cnt_ref, run_ref,
                   *, n_exp, n_grp, topk_grp, top_k, scale):
    @pl.when(pl.program_id(0) == 0)
    def _():
        run_ref[...] = jnp.zeros_like(run_ref)

    logits = lax.dot_general(rwt_ref[...], u_ref[...], (((1,), (1,)), ((), ())),
                             preferred_element_type=F32)
    tm = logits.shape[1]
    gsz = n_exp // n_grp
    scores = jax.nn.sigmoid(logits)
    sel = scores + rb_ref[...]
    neg = jnp.float32(-jnp.inf)

    grp = sel.reshape(n_grp, gsz, tm)
    io = lax.broadcasted_iota(I32, (n_grp, gsz, tm), 1)
    m1 = jnp.max(grp, axis=1, keepdims=True)
    first = jnp.min(jnp.where(grp == m1, io, gsz), axis=1, keepdims=True)
    m2 = jnp.max(jnp.where(io == first, neg, grp), axis=1, keepdims=True)
    gs = (m1 + m2).reshape(n_grp, tm)

    gi = lax.broadcasted_iota(I32, (n_grp, tm), 0)
    grank = jnp.zeros((n_grp, tm), I32)
    for g2 in range(n_grp):
        row = gs[g2:g2 + 1, :]
        beats = (row > gs) | ((row == gs) & (gi > g2))
        grank = grank + beats.astype(I32)
    gmask = (grank < topk_grp).astype(F32)
    emask = jnp.broadcast_to(gmask.reshape(n_grp, 1, tm), (n_grp, gsz, tm)).reshape(n_exp, tm) > 0.5

    msel = jnp.where(emask, sel, neg)
    ei = lax.broadcasted_iota(I32, (n_exp, tm), 0)
    rank = jnp.zeros((n_exp, tm), I32)
    for e2 in range(n_exp):
        row = msel[e2:e2 + 1, :]
        beats = (row > msel) | ((row == msel) & (ei > e2))
        rank = rank + beats.astype(I32)
    chosen = (rank < top_k) & emask

    selw = jnp.where(chosen, scores, 0.0)
    denom = jnp.sum(selw, axis=0, keepdims=True)
    gates = selw / denom * scale

    chf = jnp.where(chosen, 1.0, 0.0)
    before = jnp.dot(chf.astype(BF16), tri_ref[...], preferred_element_type=F32) + run_ref[...]
    run_new = run_ref[...] + jnp.sum(chf, axis=1, keepdims=True)
    run_ref[...] = run_new
    cnt_ref[...] = jnp.broadcast_to(run_new, cnt_ref.shape)

    eif = ei.astype(F32)
    eids, gts, rks = [], [], []
    for k in range(top_k):
        mk = chosen & (rank == k)
        eids.append(jnp.sum(jnp.where(mk, eif, 0.0), axis=0, keepdims=True))
        gts.append(jnp.sum(jnp.where(mk, gates, 0.0), axis=0, keepdims=True))
        rks.append(jnp.sum(jnp.where(mk, before, 0.0), axis=0, keepdims=True))
    eid_ref[...] = jnp.concatenate(eids, axis=0).astype(I32)
    gate_ref[...] = jnp.concatenate(gts, axis=0)
    rk_ref[...] = jnp.concatenate(rks, axis=0).astype(I32)


def _router(ub, router_w, router_bias):
    n, d = ub.shape
    n_exp = router_w.shape[1]
    tm = _pick(n, ROUTER_TILE)
    tri = jnp.asarray(np.triu(np.ones((tm, tm), np.float32), k=1), BF16)
    outs = pl.pallas_call(
        functools.partial(_router_kernel, n_exp=n_exp, n_grp=N_GROUPS, topk_grp=TOPK_GROUPS,
                          top_k=TOP_K, scale=ROUTED_SCALE),
        out_shape=[jax.ShapeDtypeStruct((TOP_K, n), I32),
                   jax.ShapeDtypeStruct((TOP_K, n), F32),
                   jax.ShapeDtypeStruct((TOP_K, n), I32),
                   jax.ShapeDtypeStruct((n_exp, LANES), F32)],
        grid=(n // tm,),
        in_specs=[pl.BlockSpec((tm, d), lambda i: (i, 0)),
                  pl.BlockSpec((n_exp, d), lambda i: (0, 0)),
                  pl.BlockSpec((n_exp, 1), lambda i: (0, 0)),
                  pl.BlockSpec((tm, tm), lambda i: (0, 0))],
        out_specs=[pl.BlockSpec((TOP_K, tm), lambda i: (0, i)),
                   pl.BlockSpec((TOP_K, tm), lambda i: (0, i)),
                   pl.BlockSpec((TOP_K, tm), lambda i: (0, i)),
                   pl.BlockSpec((n_exp, LANES), lambda i: (0, 0))],
        scratch_shapes=[pltpu.VMEM((n_exp, 1), F32)],
        compiler_params=_cp("arbitrary"),
        name="moe_router",
    )(ub, router_w.T.astype(BF16), router_bias.reshape(n_exp, 1).astype(F32), tri)
    return outs


def _dispatch_kernel(pos_ref, up_ref, xg_ref, sem, *, tm, top_k, r_words):
    base = pl.program_id(0) * tm

    def copy(t, k):
        p = pos_ref[(base + t) * top_k + k]
        return pltpu.make_async_copy(up_ref.at[pl.ds((base + t) * r_words, r_words), :],
                                     xg_ref.at[pl.ds(p * r_words, r_words), :], sem)

    def issue(t, carry):
        for k in range(top_k):
            copy(t, k).start()
        return carry

    def drain(t, carry):
        for k in range(top_k):
            copy(t, k).wait()
        return carry

    lax.fori_loop(0, tm, issue, 0)
    lax.fori_loop(0, tm, drain, 0)


def _dispatch(pos_flat, up, n, p_rows, r_words):
    tm = _pick(n, ROUTER_TILE)
    return pl.pallas_call(
        functools.partial(_dispatch_kernel, tm=tm, top_k=TOP_K, r_words=r_words),
        out_shape=jax.ShapeDtypeStruct((p_rows * r_words, LANES), U32),
        grid_spec=pltpu.PrefetchScalarGridSpec(
            num_scalar_prefetch=1,
            grid=(n // tm,),
            in_specs=[pl.BlockSpec(memory_space=pl.ANY)],
            out_specs=pl.BlockSpec(memory_space=pl.ANY),
            scratch_shapes=[pltpu.SemaphoreType.DMA]),
        compiler_params=_cp("arbitrary"),
        name="moe_dispatch",
    )(pos_flat, up)


def _expert_kernel(te_ref, tv_ref, xg_ref, wg_ref, wu_ref, wd_ref, yg_ref, *, te_rows, r_words):
    valid = tv_ref[pl.program_id(0)] != 0

    @pl.when(valid)
    def _():
        lo, hi = _load_packed(xg_ref, te_rows, r_words)
        x = jnp.concatenate(lo + hi, axis=1).astype(BF16)
        g = jnp.dot(x, wg_ref[...], preferred_element_type=F32)
        u = jnp.dot(x, wu_ref[...], preferred_element_type=F32)
        h = (g * jax.nn.sigmoid(g) * u).astype(BF16)
        y = jnp.dot(h, wd_ref[...], preferred_element_type=F32)
        _store_packed(yg_ref, y, te_rows, r_words)

    @pl.when(jnp.logical_not(valid))
    def _():
        yg_ref[...] = jnp.zeros_like(yg_ref)


def _experts(tile_e, tile_v, xg, wg, wu, wd, te_rows, r_words):
    n_tiles = tile_e.shape[0]
    _, d, ff = wg.shape
    return pl.pallas_call(
        functools.partial(_expert_kernel, te_rows=te_rows, r_words=r_words),
        out_shape=jax.ShapeDtypeStruct(xg.shape, U32),
        grid_spec=pltpu.PrefetchScalarGridSpec(
            num_scalar_prefetch=2,
            grid=(n_tiles,),
            in_specs=[pl.BlockSpec((te_rows * r_words, LANES), lambda j, te, tv: (j, 0)),
                      pl.BlockSpec((None, d, ff), lambda j, te, tv: (te[j], 0, 0)),
                      pl.BlockSpec((None, d, ff), lambda j, te, tv: (te[j], 0, 0)),
                      pl.BlockSpec((None, ff, d), lambda j, te, tv: (te[j], 0, 0))],
            out_specs=pl.BlockSpec((te_rows * r_words, LANES), lambda j, te, tv: (j, 0))),
        compiler_params=_cp("arbitrary"),
        name="moe_experts",
    )(tile_e, tile_v, xg, wg, wu, wd)


def _shared_kernel(u_ref, sg_ref, su_ref, sd_ref, o_ref):
    x = u_ref[...]
    g = jnp.dot(x, sg_ref[...], preferred_element_type=F32)
    u = jnp.dot(x, su_ref[...], preferred_element_type=F32)
    h = (g * jax.nn.sigmoid(g) * u).astype(BF16)
    o_ref[...] = jnp.dot(h, sd_ref[...], preferred_element_type=F32).astype(o_ref.dtype)


def _shared_expert(ub, sg, su, sd):
    n, d = ub.shape
    ff = sg.shape[1]
    tm = _pick(n, 512)
    return pl.pallas_call(
        _shared_kernel,
        out_shape=jax.ShapeDtypeStruct((n, d), BF16),
        grid=(n // tm,),
        in_specs=[pl.BlockSpec((tm, d), lambda i: (i, 0)),
                  pl.BlockSpec((d, ff), lambda i: (0, 0)),
                  pl.BlockSpec((d, ff), lambda i: (0, 0)),
                  pl.BlockSpec((ff, d), lambda i: (0, 0))],
        out_specs=pl.BlockSpec((tm, d), lambda i: (i, 0)),
        compiler_params=_cp("parallel"),
        name="moe_shared",
    )(ub, sg, su, sd)


def _combine_kernel(pos_ref, yg_ref, gw_ref, ysh_ref, x_ref, gate_ref, lng_ref, lnb_ref,
                    sc_ref, sh_ref, *rest, tc, top_k, r_words, alpha, kinds):
    outs, (gbuf, sem) = rest[:len(kinds)], rest[len(kinds):]
    base = pl.program_id(0) * tc

    def copy(t, k):
        p = pos_ref[(base + t) * top_k + k]
        return pltpu.make_async_copy(yg_ref.at[pl.ds(p * r_words, r_words), :],
                                     gbuf.at[k, pl.ds(t * r_words, r_words), :], sem)

    def issue(t, carry):
        for k in range(top_k):
            copy(t, k).start()
        return carry

    def drain(t, carry):
        for k in range(top_k):
            copy(t, k).wait()
        return carry

    lax.fori_loop(0, tc, issue, 0)
    lax.fori_loop(0, tc, drain, 0)

    acc_lo = [None] * r_words
    acc_hi = [None] * r_words
    for k in range(top_k):
        wk = gw_ref[:, k:k + 1]
        lo, hi = _load_packed(gbuf.at[k], tc, r_words)
        for c in range(r_words):
            acc_lo[c] = lo[c] * wk if k == 0 else acc_lo[c] + lo[c] * wk
            acc_hi[c] = hi[c] * wk if k == 0 else acc_hi[c] + hi[c] * wk
    y = jnp.concatenate(acc_lo + acc_hi, axis=1) + ysh_ref[...].astype(F32)
    z = alpha * x_ref[...] + (1.0 + gate_ref[...]) * y
    xn = _layer_norm_rows(z, lng_ref[...], lnb_ref[...])
    _emit_outputs(xn, sc_ref, sh_ref, outs, kinds, tc, r_words)


def _combine(pos_flat, yg, gate_t, ysh, x2, mod_l, gate_chunk, lng, lnb, next_mod, sc_chunk,
             sh_chunk, seq, alpha, kinds):
    n, d = x2.shape
    tc = _pick(seq, COMBINE_TILE)
    tpb = seq // tc
    r_words = d // 2 // LANES
    shapes, specs = _out_shapes_specs(kinds, n, d, tc, r_words)

    def mspec(chunk):
        return pl.BlockSpec((None, 1, d), lambda i, pos, _c=chunk: (i // tpb, 0, _c))

    return pl.pallas_call(
        functools.partial(_combine_kernel, tc=tc, top_k=TOP_K, r_words=r_words, alpha=alpha,
                          kinds=kinds),
        out_shape=shapes,
        grid_spec=pltpu.PrefetchScalarGridSpec(
            num_scalar_prefetch=1,
            grid=(n // tc,),
            in_specs=[pl.BlockSpec(memory_space=pl.ANY),
                      pl.BlockSpec((tc, TOP_K), lambda i, pos: (i, 0)),
                      pl.BlockSpec((tc, d), lambda i, pos: (i, 0)),
                      pl.BlockSpec((tc, d), lambda i, pos: (i, 0)),
                      mspec(gate_chunk),
                      pl.BlockSpec((1, d), lambda i, pos: (0, 0)),
                      pl.BlockSpec((1, d), lambda i, pos: (0, 0)),
                      mspec(sc_chunk), mspec(sh_chunk)],
            out_specs=specs,
            scratch_shapes=[pltpu.VMEM((TOP_K, tc * r_words, LANES), U32),
                            pltpu.SemaphoreType.DMA]),
        compiler_params=_cp("arbitrary"),
        name="moe_combine",
    )(pos_flat, yg, gate_t, ysh, x2, mod_l, lng.reshape(1, d), lnb.reshape(1, d),
      next_mod, next_mod)


def _moe_sublayer(x2, ub, up, mod_l, lng, lnb, next_mod, next_chunks, seq, alpha, kinds,
                  router_w, router_bias, wg, wu, wd, sg, su, sd):
    n, d = x2.shape
    n_exp = router_w.shape[1]
    r_words = d // 2 // LANES
    te_rows = _pick(n * TOP_K // n_exp, EXPERT_TILE)
    p_rows = n * TOP_K + n_exp * te_rows

    eid, gate, rk, cnt = _router(ub, router_w, router_bias)
    counts = cnt[:, 0].astype(I32)
    padded = (counts + te_rows - 1) // te_rows * te_rows
    ends = jnp.cumsum(padded)
    offs = ends - padded
    pos_flat = (jnp.take(offs, eid) + rk).T.reshape(-1)
    tile_start = jnp.arange(p_rows // te_rows, dtype=I32) * te_rows
    tile_e = jnp.minimum(jnp.searchsorted(ends, tile_start, side="right"), n_exp - 1).astype(I32)
    tile_v = (tile_start < ends[-1]).astype(I32)

    xg = _dispatch(pos_flat, up, n, p_rows, r_words)
    yg = _experts(tile_e, tile_v, xg, wg, wu, wd, te_rows, r_words)
    ysh = _shared_expert(ub, sg, su, sd)
    return _combine(pos_flat, yg, gate.T, ysh, x2, mod_l, 5, lng, lnb, next_mod,
                    next_chunks[0], next_chunks[1], seq, alpha, kinds)


def _attn_kernel(qi_ref, ki_ref, slope_ref, q_ref, k_ref, v_ref, lamp_ref, g_ref, o_ref,
                 m_ref, l_ref, acc_ref, *, tq, tk, lam_init):
    step = pl.program_id(2)
    qi = qi_ref[step]
    ki = ki_ref[step]
    slope = slope_ref[pl.program_id(1)]

    @pl.when(ki == 0)
    def _():
        m_ref[...] = jnp.full_like(m_ref, -jnp.inf)
        l_ref[...] = jnp.zeros_like(l_ref)
        acc_ref[...] = jnp.zeros_like(acc_ref)

    col = lax.broadcasted_iota(I32, (1, tk), 1)
    colbias = slope * (ki * tk - qi * tq + col).astype(F32)
    v = v_ref[...]

    def update(masked):
        if masked:
            keep = (lax.broadcasted_iota(I32, (tq, tk), 0) + qi * tq
                    >= lax.broadcasted_iota(I32, (tq, tk), 1) + ki * tk)
        for br in range(2):
            cols = slice(br * HEAD_DIM, (br + 1) * HEAD_DIM)
            s = lax.dot_general(q_ref[:, cols], k_ref[:, cols], (((1,), (1,)), ((), ())),
                                preferred_element_type=F32) + colbias
            if masked:
                s = jnp.where(keep, s, -jnp.inf)
            m_prev = m_ref[br]
            m_new = jnp.maximum(m_prev, jnp.max(s, axis=-1, keepdims=True))
            a = jnp.exp(m_prev - m_new)
            p = jnp.exp(s - m_new)
            l_ref[br] = a * l_ref[br] + jnp.sum(p, axis=-1, keepdims=True)
            acc_ref[br] = a * acc_ref[br] + jnp.dot(p.astype(BF16), v, preferred_element_type=F32)
            m_ref[br] = m_new

    @pl.when(ki < qi)
    def _():
        update(False)

    @pl.when(ki == qi)
    def _():
        update(True)
        lp = lamp_ref[...]
        lam = (jnp.exp(jnp.sum(lp[0:1] * lp[1:2], axis=-1, keepdims=True))
               - jnp.exp(jnp.sum(lp[2:3] * lp[3:4], axis=-1, keepdims=True)) + lam_init)
        o = acc_ref[0] / l_ref[0] - lam * (acc_ref[1] / l_ref[1])
        o = o * lax.rsqrt(jnp.mean(o * o, axis=-1, keepdims=True) + LN_EPS) * g_ref[...]
        o_ref[...] = (o * (1.0 - lam_init)).astype(o_ref.dtype)


def _diff_attention(q, k, v, lam_p, subln_g, batch, seq, lam_init):
    n, d = q.shape
    dv = 2 * HEAD_DIM
    n_heads = d // dv
    t = _pick(seq, ATTN_TILE)
    nq = seq // t
    pairs = [(a, b) for a in range(nq) for b in range(a + 1)]
    qi = jnp.asarray([p[0] for p in pairs], I32)
    ki = jnp.asarray([p[1] for p in pairs], I32)
    slopes = 2.0 ** (-8.0 * jnp.arange(1, n_heads + 1, dtype=F32) / n_heads)
    q3, k3, v3 = (a.reshape(batch, seq, d) for a in (q, k, v))
    out = pl.pallas_call(
        functools.partial(_attn_kernel, tq=t, tk=t, lam_init=lam_init),
        out_shape=jax.ShapeDtypeStruct((batch, seq, d), BF16),
        grid_spec=pltpu.PrefetchScalarGridSpec(
            num_scalar_prefetch=3,
            grid=(batch, n_heads, len(pairs)),
            in_specs=[pl.BlockSpec((None, t, dv), lambda b, h, s, qi, ki, sl: (b, qi[s], h)),
                      pl.BlockSpec((None, t, dv), lambda b, h, s, qi, ki, sl: (b, ki[s], h)),
                      pl.BlockSpec((None, t, dv), lambda b, h, s, qi, ki, sl: (b, ki[s], h)),
                      pl.BlockSpec((4, HEAD_DIM), lambda b, h, s, qi, ki, sl: (0, 0)),
                      pl.BlockSpec((1, dv), lambda b, h, s, qi, ki, sl: (0, 0))],
            out_specs=pl.BlockSpec((None, t, dv), lambda b, h, s, qi, ki, sl: (b, qi[s], h)),
            scratch_shapes=[pltpu.VMEM((2, t, 1), F32), pltpu.VMEM((2, t, 1), F32),
                            pltpu.VMEM((2, t, dv), F32)]),
        compiler_params=_cp("parallel", "parallel", "arbitrary"),
        name="diff_attention",
    )(qi, ki, slopes, q3, k3, v3, lam_p.astype(F32), subln_g.reshape(1, dv).astype(F32))
    return out.reshape(n, d)


def kernel(x, c, ada_w, ada_b, ln_g, ln_b, conv_pw1_w, conv_pw1_b, conv_dw_w, conv_dw_b,
           conv_ln_g, conv_ln_b, conv_pw2_w, conv_pw2_b, attn_wk, attn_wv, attn_wq,
           attn_lambda, attn_subln_g, attn_wo, router_w, router_bias, exp_w_gate, exp_w_up,
           exp_w_down, shared_w_gate, shared_w_up, shared_w_down):
    batch, seq, d = x.shape
    depth = ada_w.shape[0]
    n_a = conv_pw1_w.shape[0]
    n = batch * seq
    alpha = (2 * depth) ** 0.25
    zeros_d = jnp.zeros((d,), F32)

    mod = _ada_mod(c, ada_w, ada_b)
    mods = [mod[l].reshape(batch, 1, 6 * d) for l in range(depth)]
    x2 = x.reshape(n, d)
    xb = None
    ub = _modulate(x2, mods[0], seq, 1, 0)
    kb = vb = None

    for l in range(depth):
        mod_l = mods[l]
        if l < n_a:
            h = _matmul_glu(ub, conv_pw1_w[l].astype(BF16), conv_pw1_b[l])
            h = _conv_ln_swish(h, conv_dw_w[l], conv_dw_b[l], conv_ln_g[l], conv_ln_b[l],
                               batch, seq)
            y = _matmul(h, conv_pw2_w[l].astype(BF16), conv_pw2_b[l])
        else:
            j = l - n_a
            if l == n_a:
                kb = _matmul(xb, attn_wk.astype(BF16), zeros_d)
                vb = _matmul(xb, attn_wv.astype(BF16), zeros_d)
            qb = _matmul(ub, attn_wq[j].astype(BF16), zeros_d, scale=HEAD_DIM ** -0.5)
            lam_init = 0.8 - 0.6 * math.exp(-0.3 * l)
            o = _diff_attention(qb, kb, vb, attn_lambda[j], attn_subln_g[j], batch, seq, lam_init)
            y = _matmul(o, attn_wo[j].astype(BF16), zeros_d)
        x2, ub, up = _post(x2, y, mod_l, 2, ln_g[l, 0], ln_b[l, 0], mod_l, 4, 3, seq, alpha,
                           ("x", "ub", "up"))
        last = l == depth - 1
        kinds = ("x",) if last else (("x", "xb", "ub") if l + 1 >= n_a else ("x", "ub"))
        next_mod = mod_l if last else mods[l + 1]
        outs = _moe_sublayer(
            x2, ub, up, mod_l, ln_g[l, 1], ln_b[l, 1], next_mod, (1, 0), seq, alpha, kinds,
            router_w[l], router_bias[l], exp_w_gate[l].astype(BF16), exp_w_up[l].astype(BF16),
            exp_w_down[l].astype(BF16), shared_w_gate[l].astype(BF16),
            shared_w_up[l].astype(BF16), shared_w_down[l].astype(BF16))
        x2 = outs[0]
        if not last:
            ub = outs[-1]
            xb = outs[1] if len(outs) == 3 else None
    return x2.reshape(batch, seq, d)
```

```python
import functools
import math

import jax
import jax.numpy as jnp
import numpy as np
from jax import lax
from jax.experimental import pallas as pl
from jax.experimental.pallas import tpu as pltpu

F32 = jnp.float32
BF16 = jnp.bfloat16
U32 = jnp.uint32
I32 = jnp.int32

HEAD_DIM = 128
N_GROUPS = 8
TOPK_GROUPS = 4
TOP_K = 8
ROUTED_SCALE = 2.5
LN_EPS = 1e-5
LOG2E = math.log2(math.e)

LANES = 128
ATTN_TILE = 1024
ATTN_ROWS = 256
ATTN_COLS = 512
EXPERT_TILE = 256
ROUTER_TILE = 512
COMBINE_TILE = 128
VMEM_LIMIT = 56 * 1024 * 1024


def _cp(*sem):
    return pltpu.CompilerParams(dimension_semantics=sem, vmem_limit_bytes=VMEM_LIMIT)


def _pick(n, pref):
    t = min(n, pref)
    while n % t:
        t //= 2
    return t


def _bf16_bits(v):
    return lax.bitcast_convert_type(v.astype(BF16).astype(F32), U32)


def _store_packed(ref, val, rows, r_words):
    half = r_words * LANES
    for c in range(r_words):
        lo = _bf16_bits(val[:, c * LANES:(c + 1) * LANES])
        hi = _bf16_bits(val[:, half + c * LANES:half + (c + 1) * LANES])
        ref[pl.ds(c, rows, stride=r_words), :] = (lo >> 16) | hi


def _load_packed(ref, rows, r_words):
    lo, hi = [], []
    for c in range(r_words):
        w = ref[pl.ds(c, rows, stride=r_words), :]
        lo.append(lax.bitcast_convert_type(w << 16, F32))
        hi.append(lax.bitcast_convert_type(w & jnp.uint32(0xFFFF0000), F32))
    return lo, hi


def _ada_kernel(c_ref, w_ref, b_ref, o_ref, cs_ref, *, nb, tn):
    @pl.when((pl.program_id(0) == 0) & (pl.program_id(1) == 0))
    def _():
        c = c_ref[...]
        cs_ref[...] = c * jax.nn.sigmoid(c)

    for b in range(nb):
        parts = []
        for j in range(tn // LANES):
            prod = w_ref[:, j * LANES:(j + 1) * LANES] * cs_ref[b]
            parts.append(jnp.sum(prod, axis=0, keepdims=True))
        o_ref[b:b + 1, :] = jnp.concatenate(parts, axis=1) + b_ref[...]


def _ada_mod(c, ada_w, ada_b):
    nl, d, n6 = ada_w.shape
    nb = c.shape[0]
    tn = _pick(n6, 512)
    c_b = jnp.broadcast_to(c[:, :, None], (nb, d, LANES))
    return pl.pallas_call(
        functools.partial(_ada_kernel, nb=nb, tn=tn),
        out_shape=jax.ShapeDtypeStruct((nl, nb, n6), F32),
        grid=(nl, n6 // tn),
        in_specs=[
            pl.BlockSpec((nb, d, LANES), lambda l, j: (0, 0, 0)),
            pl.BlockSpec((None, d, tn), lambda l, j: (l, 0, j)),
            pl.BlockSpec((None, 1, tn), lambda l, j: (l, 0, j)),
        ],
        out_specs=pl.BlockSpec((None, nb, tn), lambda l, j: (l, 0, j)),
        scratch_shapes=[pltpu.VMEM((nb, d, LANES), F32)],
        compiler_params=_cp("arbitrary", "arbitrary"),
        name="ada_mod",
    )(c_b, ada_w, ada_b.reshape(nl, 1, n6))


def _layer_norm_rows(z, g, b):
    mu = jnp.mean(z, axis=-1, keepdims=True)
    zc = z - mu
    var = jnp.mean(zc * zc, axis=-1, keepdims=True)
    return zc * lax.rsqrt(var + LN_EPS) * g + b


def _mod_spec(d, tiles_per_batch, chunk):
    return pl.BlockSpec((None, 1, d), lambda i, _c=chunk, _t=tiles_per_batch: (i // _t, 0, _c))


def _modulate_kernel(x_ref, sc_ref, sh_ref, u_ref):
    u_ref[...] = (x_ref[...] * (1.0 + sc_ref[...]) + sh_ref[...]).astype(u_ref.dtype)


def _modulate(x2, mod_l, seq, sc_chunk, sh_chunk):
    n, d = x2.shape
    ts = _pick(seq, 512)
    tpb = seq // ts
    return pl.pallas_call(
        _modulate_kernel,
        out_shape=jax.ShapeDtypeStruct((n, d), BF16),
        grid=(n // ts,),
        in_specs=[pl.BlockSpec((ts, d), lambda i: (i, 0)),
                  _mod_spec(d, tpb, sc_chunk), _mod_spec(d, tpb, sh_chunk)],
        out_specs=pl.BlockSpec((ts, d), lambda i: (i, 0)),
        compiler_params=_cp("parallel"),
        name="modulate",
    )(x2, mod_l, mod_l)


def _mm_kernel(a_ref, w_ref, b_ref, o_ref, *, scale):
    acc = jnp.dot(a_ref[...], w_ref[...], preferred_element_type=F32) + b_ref[...]
    if scale != 1.0:
        acc = acc * scale
    o_ref[...] = acc.astype(o_ref.dtype)


def _matmul(a, w, bias, *, scale=1.0, out_dtype=BF16):
    m, k = a.shape
    n = w.shape[1]
    tm, tn = _pick(m, 1024), _pick(n, 1024)
    return pl.pallas_call(
        functools.partial(_mm_kernel, scale=scale),
        out_shape=jax.ShapeDtypeStruct((m, n), out_dtype),
        grid=(m // tm, n // tn),
        in_specs=[pl.BlockSpec((tm, k), lambda i, j: (i, 0)),
                  pl.BlockSpec((k, tn), lambda i, j: (0, j)),
                  pl.BlockSpec((1, tn), lambda i, j: (0, j))],
        out_specs=pl.BlockSpec((tm, tn), lambda i, j: (i, j)),
        compiler_params=_cp("parallel", "parallel"),
        name="matmul",
    )(a, w, bias.reshape(1, n))


def _glu_kernel(a_ref, wa_ref, wg_ref, ba_ref, bg_ref, o_ref):
    a = a_ref[...]
    lin = jnp.dot(a, wa_ref[...], preferred_element_type=F32) + ba_ref[...]
    gate = jnp.dot(a, wg_ref[...], preferred_element_type=F32) + bg_ref[...]
    o_ref[...] = (lin * jax.nn.sigmoid(gate)).astype(o_ref.dtype)


def _matmul_glu(a, w, bias):
    m, k = a.shape
    n = w.shape[1] // 2
    tm, tn = _pick(m, 1024), _pick(n, 512)
    nj = n // tn
    b2 = bias.reshape(1, 2 * n)
    return pl.pallas_call(
        _glu_kernel,
        out_shape=jax.ShapeDtypeStruct((m, n), BF16),
        grid=(m // tm, nj),
        in_specs=[pl.BlockSpec((tm, k), lambda i, j: (i, 0)),
                  pl.BlockSpec((k, tn), lambda i, j: (0, j)),
                  pl.BlockSpec((k, tn), lambda i, j, _nj=nj: (0, j + _nj)),
                  pl.BlockSpec((1, tn), lambda i, j: (0, j)),
                  pl.BlockSpec((1, tn), lambda i, j, _nj=nj: (0, j + _nj))],
        out_specs=pl.BlockSpec((tm, tn), lambda i, j: (i, j)),
        compiler_params=_cp("parallel", "parallel"),
        name="matmul_glu",
    )(a, w, w, b2, b2)


def _conv_kernel(cur_ref, halo_ref, w_ref, dwb_ref, g_ref, b_ref, o_ref, buf_ref, cv_ref,
                 *, ts, hb, kw, rc, cw):
    d = cur_ref.shape[-1]
    first = pl.program_id(1) == 0
    buf_ref[0:hb, :] = jnp.where(first, 0.0, halo_ref[...].astype(F32))
    buf_ref[hb:hb + ts, :] = cur_ref[...].astype(F32)
    for r in range(ts // rc):
        for c in range(d // cw):
            cols = slice(c * cw, (c + 1) * cw)
            acc = jnp.zeros((rc, cw), F32)
            for k in range(kw):
                off = hb - (kw - 1) + k + r * rc
                acc = acc + buf_ref[off:off + rc, cols] * w_ref[k:k + 1, cols]
            cv_ref[r * rc:(r + 1) * rc, cols] = acc + dwb_ref[:, cols]
    y = _layer_norm_rows(cv_ref[...], g_ref[...], b_ref[...])
    o_ref[...] = (y * jax.nn.sigmoid(y)).astype(o_ref.dtype)


def _conv_ln_swish(h, dw_w, dw_b, g, b, batch, seq):
    n, d = h.shape
    kw = dw_w.shape[0]
    hb = 32
    assert kw - 1 <= hb and seq % hb == 0
    ts = _pick(seq, 128)
    rc, cw = _pick(ts, 64), _pick(d, 256)
    h3 = h.reshape(batch, seq, d)
    bpt = ts // hb
    out = pl.pallas_call(
        functools.partial(_conv_kernel, ts=ts, hb=hb, kw=kw, rc=rc, cw=cw),
        out_shape=jax.ShapeDtypeStruct((batch, seq, d), BF16),
        grid=(batch, seq // ts),
        in_specs=[pl.BlockSpec((None, ts, d), lambda bi, i: (bi, i, 0)),
                  pl.BlockSpec((None, hb, d),
                               lambda bi, i, _b=bpt: (bi, jnp.maximum(i * _b - 1, 0), 0)),
                  pl.BlockSpec((kw, d), lambda bi, i: (0, 0)),
                  pl.BlockSpec((1, d), lambda bi, i: (0, 0)),
                  pl.BlockSpec((1, d), lambda bi, i: (0, 0)),
                  pl.BlockSpec((1, d), lambda bi, i: (0, 0))],
        out_specs=pl.BlockSpec((None, ts, d), lambda bi, i: (bi, i, 0)),
        scratch_shapes=[pltpu.VMEM((hb + ts, d), F32), pltpu.VMEM((ts, d), F32)],
        compiler_params=_cp("parallel", "parallel"),
        name="dwconv_ln_swish",
    )(h3, h3, dw_w, dw_b.reshape(1, d), g.reshape(1, d), b.reshape(1, d))
    return out.reshape(n, d)


def _emit_outputs(xn, sc_ref, sh_ref, outs, kinds, rows, r_words):
    u = None
    for ref, kind in zip(outs, kinds):
        if kind == "x":
            ref[...] = xn
        elif kind == "xb":
            ref[...] = xn.astype(BF16)
        else:
            if u is None:
                u = xn * (1.0 + sc_ref[...]) + sh_ref[...]
            if kind == "ub":
                ref[...] = u.astype(BF16)
            else:
                _store_packed(ref, u, rows, r_words)


def _post_kernel(x_ref, y_ref, gate_ref, lng_ref, lnb_ref, sc_ref, sh_ref, *outs,
                 alpha, kinds, rows, r_words):
    z = alpha * x_ref[...] + (1.0 + gate_ref[...]) * y_ref[...].astype(F32)
    xn = _layer_norm_rows(z, lng_ref[...], lnb_ref[...])
    _emit_outputs(xn, sc_ref, sh_ref, outs, kinds, rows, r_words)


def _out_shapes_specs(kinds, n, d, ts, r_words):
    shapes, specs = [], []
    for kind in kinds:
        if kind == "x":
            shapes.append(jax.ShapeDtypeStruct((n, d), F32))
            specs.append(pl.BlockSpec((ts, d), lambda i, *_: (i, 0)))
        elif kind in ("xb", "ub"):
            shapes.append(jax.ShapeDtypeStruct((n, d), BF16))
            specs.append(pl.BlockSpec((ts, d), lambda i, *_: (i, 0)))
        else:
            shapes.append(jax.ShapeDtypeStruct((n * r_words, LANES), U32))
            specs.append(pl.BlockSpec((ts * r_words, LANES), lambda i, *_: (i, 0)))
    return shapes, specs


def _post(x2, y, mod_l, gate_chunk, lng, lnb, next_mod, sc_chunk, sh_chunk, seq, alpha, kinds):
    n, d = x2.shape
    ts = _pick(seq, 256)
    tpb = seq // ts
    r_words = d // 2 // LANES
    shapes, specs = _out_shapes_specs(kinds, n, d, ts, r_words)
    return pl.pallas_call(
        functools.partial(_post_kernel, alpha=alpha, kinds=kinds, rows=ts, r_words=r_words),
        out_shape=shapes,
        grid=(n // ts,),
        in_specs=[pl.BlockSpec((ts, d), lambda i: (i, 0)),
                  pl.BlockSpec((ts, d), lambda i: (i, 0)),
                  _mod_spec(d, tpb, gate_chunk),
                  pl.BlockSpec((1, d), lambda i: (0, 0)),
                  pl.BlockSpec((1, d), lambda i: (0, 0)),
                  _mod_spec(d, tpb, sc_chunk), _mod_spec(d, tpb, sh_chunk)],
        out_specs=specs,
        compiler_params=_cp("parallel"),
        name="residual_ln",
    )(x2, y, mod_l, lng.reshape(1, d), lnb.reshape(1, d), next_mod, next_mod)


def _router_kernel(u_ref, rwt_ref, rb_ref, tri_ref, eid_ref, gate_ref, rk_ref, cnt_ref, run_ref,
                   *, n_exp, n_grp, topk_grp, top_k, scale):
    @pl.when(pl.program_id(0) == 0)
    def _():
        run_ref[...] = jnp.zeros_like(run_ref)

    logits = lax.dot_general(rwt_ref[...], u_ref[...], (((1,), (1,)), ((), ())),
                             preferred_element_type=F32)
    tm = logits.shape[1]
    gsz = n_exp // n_grp
    scores = jax.nn.sigmoid(logits)
    sel = scores + rb_ref[...]
    neg = jnp.float32(-jnp.inf)

    grp = sel.reshape(n_grp, gsz, tm)
    io = lax.broadcasted_iota(I32, (n_grp, gsz, tm), 1)
    m1 = jnp.max(grp, axis=1, keepdims=True)
    first = jnp.min(jnp.where(grp == m1, io, gsz), axis=1, keepdims=True)
    m2 = jnp.max(jnp.where(io == first, neg, grp), axis=1, keepdims=True)
    gs = (m1 + m2).reshape(n_grp, tm)

    gi = lax.broadcasted_iota(I32, (n_grp, tm), 0)
    grank = jnp.zeros((n_grp, tm), I32)
    for g2 in range(n_grp):
        row = gs[g2:g2 + 1, :]
        beats = (row > gs) | ((row == gs) & (gi > g2))
        grank = grank + beats.astype(I32)
    gmask = (grank < topk_grp).astype(F32)
    emask = jnp.broadcast_to(gmask.reshape(n_grp, 1, tm), (n_grp, gsz, tm)).reshape(n_exp, tm) > 0.5

    msel = jnp.where(emask, sel, neg)
    ei = lax.broadcasted_iota(I32, (n_exp, tm), 0)
    rank = jnp.zeros((n_exp, tm), I32)
    for e2 in range(n_exp):
        row = msel[e2:e2 + 1, :]
        beats = (row > msel) | ((row == msel) & (ei > e2))
        rank = rank + beats.astype(I32)
    chosen = (rank < top_k) & emask

    selw = jnp.where(chosen, scores, 0.0)
    denom = jnp.sum(selw, axis=0, keepdims=True)
    gates = selw / denom * scale

    chf = jnp.where(chosen, 1.0, 0.0)
    before = jnp.dot(chf.astype(BF16), tri_ref[...], preferred_element_type=F32) + run_ref[...]
    run_new = run_ref[...] + jnp.sum(chf, axis=1, keepdims=True)
    run_ref[...] = run_new
    cnt_ref[...] = jnp.broadcast_to(run_new, cnt_ref.shape)

    eif = ei.astype(F32)
    eids, gts, rks = [], [], []
    for k in range(top_k):
        mk = chosen & (rank == k)
        eids.append(jnp.sum(jnp.where(mk, eif, 0.0), axis=0, keepdims=True))
        gts.append(jnp.sum(jnp.where(mk, gates, 0.0), axis=0, keepdims=True))
        rks.append(jnp.sum(jnp.where(mk, before, 0.0), axis=0, keepdims=True))
    eid_ref[...] = jnp.concatenate(eids, axis=0).astype(I32)
    gate_ref[...] = jnp.concatenate(gts, axis=0)
    rk_ref[...] = jnp.concatenate(rks, axis=0).astype(I32)


def _router(ub, router_w, router_bias):
    n, d = ub.shape
    n_exp = router_w.shape[1]
    tm = _pick(n, ROUTER_TILE)
    tri = jnp.asarray(np.triu(np.ones((tm, tm), np.float32), k=1), BF16)
    outs = pl.pallas_call(
        functools.partial(_router_kernel, n_exp=n_exp, n_grp=N_GROUPS, topk_grp=TOPK_GROUPS,
                          top_k=TOP_K, scale=ROUTED_SCALE),
        out_shape=[jax.ShapeDtypeStruct((TOP_K, n), I32),
                   jax.ShapeDtypeStruct((TOP_K, n), F32),
                   jax.ShapeDtypeStruct((TOP_K, n), I32),
                   jax.ShapeDtypeStruct((n_exp, LANES), F32)],
        grid=(n // tm,),
        in_specs=[pl.BlockSpec((tm, d), lambda i: (i, 0)),
                  pl.BlockSpec((n_exp, d), lambda i: (0, 0)),
                  pl.BlockSpec((n_exp, 1), lambda i: (0, 0)),
                  pl.BlockSpec((tm, tm), lambda i: (0, 0))],
        out_specs=[pl.BlockSpec((TOP_K, tm), lambda i: (0, i)),
                   pl.BlockSpec((TOP_K, tm), lambda i: (0, i)),
                   pl.BlockSpec((TOP_K, tm), lambda i: (0, i)),
                   pl.BlockSpec((n_exp, LANES), lambda i: (0, 0))],
        scratch_shapes=[pltpu.VMEM((n_exp, 1), F32)],
        compiler_params=_cp("arbitrary"),
        name="moe_router",
    )(ub, router_w.T.astype(BF16), router_bias.reshape(n_exp, 1).astype(F32), tri)
    return outs


def _dispatch_kernel(pos_ref, up_ref, xg_ref, sem, *, tm, top_k, r_words):
    base = pl.program_id(0) * tm

    def copy(t, k):
        p = pos_ref[(base + t) * top_k + k]
        return pltpu.make_async_copy(up_ref.at[pl.ds(t * r_words, r_words), :],
                                     xg_ref.at[pl.ds(p * r_words, r_words), :], sem)

    def issue(t, carry):
        for k in range(top_k):
            copy(t, k).start()
        return carry

    def drain(t, carry):
        for k in range(top_k):
            copy(t, k).wait()
        return carry

    lax.fori_loop(0, tm, issue, 0)
    lax.fori_loop(0, tm, drain, 0)


def _dispatch(pos_flat, up, n, p_rows, r_words):
    tm = _pick(n, ROUTER_TILE)
    return pl.pallas_call(
        functools.partial(_dispatch_kernel, tm=tm, top_k=TOP_K, r_words=r_words),
        out_shape=jax.ShapeDtypeStruct((p_rows * r_words, LANES), U32),
        grid_spec=pltpu.PrefetchScalarGridSpec(
            num_scalar_prefetch=1,
            grid=(n // tm,),
            in_specs=[pl.BlockSpec((tm * r_words, LANES), lambda i, pos: (i, 0))],
            out_specs=pl.BlockSpec(memory_space=pl.ANY),
            scratch_shapes=[pltpu.SemaphoreType.DMA]),
        compiler_params=_cp("arbitrary"),
        name="moe_dispatch",
    )(pos_flat, up)


def _expert_kernel(te_ref, tv_ref, xg_ref, wg_ref, wu_ref, wd_ref, yg_ref, *, te_rows, r_words):
    valid = tv_ref[pl.program_id(0)] != 0

    @pl.when(valid)
    def _():
        lo, hi = _load_packed(xg_ref, te_rows, r_words)
        x = jnp.concatenate(lo + hi, axis=1).astype(BF16)
        g = jnp.dot(x, wg_ref[...], preferred_element_type=F32)
        u = jnp.dot(x, wu_ref[...], preferred_element_type=F32)
        h = (g * jax.nn.sigmoid(g) * u).astype(BF16)
        y = jnp.dot(h, wd_ref[...], preferred_element_type=F32)
        _store_packed(yg_ref, y, te_rows, r_words)

    @pl.when(jnp.logical_not(valid))
    def _():
        yg_ref[...] = jnp.zeros_like(yg_ref)


def _experts(tile_e, tile_v, xg, wg, wu, wd, te_rows, r_words):
    n_tiles = tile_e.shape[0]
    _, d, ff = wg.shape
    return pl.pallas_call(
        functools.partial(_expert_kernel, te_rows=te_rows, r_words=r_words),
        out_shape=jax.ShapeDtypeStruct(xg.shape, U32),
        grid_spec=pltpu.PrefetchScalarGridSpec(
            num_scalar_prefetch=2,
            grid=(n_tiles,),
            in_specs=[pl.BlockSpec((te_rows * r_words, LANES), lambda j, te, tv: (j, 0)),
                      pl.BlockSpec((None, d, ff), lambda j, te, tv: (te[j], 0, 0)),
                      pl.BlockSpec((None, d, ff), lambda j, te, tv: (te[j], 0, 0)),
                      pl.BlockSpec((None, ff, d), lambda j, te, tv: (te[j], 0, 0))],
            out_specs=pl.BlockSpec((te_rows * r_words, LANES), lambda j, te, tv: (j, 0))),
        compiler_params=_cp("arbitrary"),
        name="moe_experts",
    )(tile_e, tile_v, xg, wg, wu, wd)


def _shared_kernel(u_ref, sg_ref, su_ref, sd_ref, o_ref):
    x = u_ref[...]
    g = jnp.dot(x, sg_ref[...], preferred_element_type=F32)
    u = jnp.dot(x, su_ref[...], preferred_element_type=F32)
    h = (g * jax.nn.sigmoid(g) * u).astype(BF16)
    o_ref[...] = jnp.dot(h, sd_ref[...], preferred_element_type=F32).astype(o_ref.dtype)


def _shared_expert(ub, sg, su, sd):
    n, d = ub.shape
    ff = sg.shape[1]
    tm = _pick(n, 512)
    return pl.pallas_call(
        _shared_kernel,
        out_shape=jax.ShapeDtypeStruct((n, d), BF16),
        grid=(n // tm,),
        in_specs=[pl.BlockSpec((tm, d), lambda i: (i, 0)),
                  pl.BlockSpec((d, ff), lambda i: (0, 0)),
                  pl.BlockSpec((d, ff), lambda i: (0, 0)),
                  pl.BlockSpec((ff, d), lambda i: (0, 0))],
        out_specs=pl.BlockSpec((tm, d), lambda i: (i, 0)),
        compiler_params=_cp("parallel"),
        name="moe_shared",
    )(ub, sg, su, sd)


def _combine_kernel(pos_ref, yg_ref, gw_ref, ysh_ref, x_ref, gate_ref, lng_ref, lnb_ref,
                    sc_ref, sh_ref, *rest, tc, top_k, r_words, alpha, kinds):
    outs, (gbuf, sem) = rest[:len(kinds)], rest[len(kinds):]
    base = pl.program_id(0) * tc

    def copy(t, k):
        p = pos_ref[(base + t) * top_k + k]
        return pltpu.make_async_copy(yg_ref.at[pl.ds(p * r_words, r_words), :],
                                     gbuf.at[k, pl.ds(t * r_words, r_words), :], sem)

    def issue(t, carry):
        for k in range(top_k):
            copy(t, k).start()
        return carry

    def drain(t, carry):
        for k in range(top_k):
            copy(t, k).wait()
        return carry

    lax.fori_loop(0, tc, issue, 0)
    lax.fori_loop(0, tc, drain, 0)

    acc_lo = [None] * r_words
    acc_hi = [None] * r_words
    for k in range(top_k):
        wk = gw_ref[:, k:k + 1]
        lo, hi = _load_packed(gbuf.at[k], tc, r_words)
        for c in range(r_words):
            acc_lo[c] = lo[c] * wk if k == 0 else acc_lo[c] + lo[c] * wk
            acc_hi[c] = hi[c] * wk if k == 0 else acc_hi[c] + hi[c] * wk
    y = jnp.concatenate(acc_lo + acc_hi, axis=1) + ysh_ref[...].astype(F32)
    z = alpha * x_ref[...] + (1.0 + gate_ref[...]) * y
    xn = _layer_norm_rows(z, lng_ref[...], lnb_ref[...])
    _emit_outputs(xn, sc_ref, sh_ref, outs, kinds, tc, r_words)


def _combine(pos_flat, yg, gate_t, ysh, x2, mod_l, gate_chunk, lng, lnb, next_mod, sc_chunk,
             sh_chunk, seq, alpha, kinds):
    n, d = x2.shape
    tc = _pick(seq, COMBINE_TILE)
    tpb = seq // tc
    r_words = d // 2 // LANES
    shapes, specs = _out_shapes_specs(kinds, n, d, tc, r_words)

    def mspec(chunk):
        return pl.BlockSpec((None, 1, d), lambda i, pos, _c=chunk: (i // tpb, 0, _c))

    return pl.pallas_call(
        functools.partial(_combine_kernel, tc=tc, top_k=TOP_K, r_words=r_words, alpha=alpha,
                          kinds=kinds),
        out_shape=shapes,
        grid_spec=pltpu.PrefetchScalarGridSpec(
            num_scalar_prefetch=1,
            grid=(n // tc,),
            in_specs=[pl.BlockSpec(memory_space=pl.ANY),
                      pl.BlockSpec((tc, TOP_K), lambda i, pos: (i, 0)),
                      pl.BlockSpec((tc, d), lambda i, pos: (i, 0)),
                      pl.BlockSpec((tc, d), lambda i, pos: (i, 0)),
                      mspec(gate_chunk),
                      pl.BlockSpec((1, d), lambda i, pos: (0, 0)),
                      pl.BlockSpec((1, d), lambda i, pos: (0, 0)),
                      mspec(sc_chunk), mspec(sh_chunk)],
            out_specs=specs,
            scratch_shapes=[pltpu.VMEM((TOP_K, tc * r_words, LANES), U32),
                            pltpu.SemaphoreType.DMA]),
        compiler_params=_cp("arbitrary"),
        name="moe_combine",
    )(pos_flat, yg, gate_t, ysh, x2, mod_l, lng.reshape(1, d), lnb.reshape(1, d),
      next_mod, next_mod)


def _moe_sublayer(x2, ub, up, mod_l, lng, lnb, next_mod, next_chunks, seq, alpha, kinds,
                  router_w, router_bias, wg, wu, wd, sg, su, sd):
    n, d = x2.shape
    n_exp = router_w.shape[1]
    r_words = d // 2 // LANES
    te_rows = _pick(n * TOP_K // n_exp, EXPERT_TILE)
    p_rows = n * TOP_K + n_exp * te_rows

    eid, gate, rk, cnt = _router(ub, router_w, router_bias)
    counts = cnt[:, 0].astype(I32)
    padded = (counts + te_rows - 1) // te_rows * te_rows
    ends = jnp.cumsum(padded)
    offs = ends - padded
    onehot = eid[:, :, None] == jnp.arange(n_exp, dtype=I32)
    pos_flat = (jnp.sum(jnp.where(onehot, offs, 0), axis=-1) + rk).T.reshape(-1)
    tile_start = jnp.arange(p_rows // te_rows, dtype=I32) * te_rows
    tile_e = jnp.minimum(jnp.sum((ends[None, :] <= tile_start[:, None]).astype(I32), axis=1),
                         n_exp - 1)
    tile_v = (tile_start < ends[-1]).astype(I32)

    xg = _dispatch(pos_flat, up, n, p_rows, r_words)
    yg = _experts(tile_e, tile_v, xg, wg, wu, wd, te_rows, r_words)
    ysh = _shared_expert(ub, sg, su, sd)
    return _combine(pos_flat, yg, gate.T, ysh, x2, mod_l, 5, lng, lnb, next_mod,
                    next_chunks[0], next_chunks[1], seq, alpha, kinds)


def _attn_kernel(qi_ref, ki_ref, slope_ref, q_ref, k_ref, v_ref, lamp_ref, g_ref, o_ref,
                 m_ref, l_ref, acc_ref, s_ref, p_ref, *, tq, tk, rq, ck, lam_init):
    step = pl.program_id(2)
    qi = qi_ref[step]
    ki = ki_ref[step]
    slope = slope_ref[pl.program_id(1)]

    @pl.when(ki == 0)
    def _():
        m_ref[...] = jnp.full_like(m_ref, -jnp.inf)
        l_ref[...] = jnp.zeros_like(l_ref)
        acc_ref[...] = jnp.zeros_like(acc_ref)

    col = lax.broadcasted_iota(I32, (1, tk), 1)
    colbias = (slope * LOG2E) * (ki * tk - qi * tq + col).astype(F32)

    def fold(x, op):
        out = x[:, 0:LANES]
        for j in range(1, x.shape[1] // LANES):
            out = op(out, x[:, j * LANES:(j + 1) * LANES])
        return out

    def update(masked):
        for rb in range(tq // rq):
            rows = slice(rb * rq, (rb + 1) * rq)
            n_ck = ((rb + 1) * rq + ck - 1) // ck if masked else tk // ck
            kw = n_ck * ck
            for br in range(2):
                cols = slice(br * HEAD_DIM, (br + 1) * HEAD_DIM)
                q_blk = q_ref[rows, cols]
                mrun = None
                for cc in range(n_ck):
                    kcols = slice(cc * ck, (cc + 1) * ck)
                    s = lax.dot_general(q_blk, k_ref[kcols, cols], (((1,), (1,)), ((), ())),
                                        preferred_element_type=F32) + colbias[:, kcols]
                    if masked and (cc + 1) * ck > rb * rq + 1:
                        keep = (lax.broadcasted_iota(I32, (rq, ck), 0) + rb * rq
                                >= lax.broadcasted_iota(I32, (rq, ck), 1) + cc * ck)
                        s = jnp.where(keep, s, -jnp.inf)
                    s_ref[:, kcols] = s
                    part = fold(s, jnp.maximum)
                    mrun = part if mrun is None else jnp.maximum(mrun, part)
                m_prev = m_ref[br, rows]
                m_new = jnp.maximum(m_prev, jnp.max(mrun, axis=-1, keepdims=True))
                a = jnp.exp2(m_prev - m_new)
                lrun = None
                for cc in range(n_ck):
                    kcols = slice(cc * ck, (cc + 1) * ck)
                    p = jnp.exp2(s_ref[:, kcols] - m_new)
                    p_ref[:, kcols] = p.astype(BF16)
                    part = fold(p, jnp.add)
                    lrun = part if lrun is None else lrun + part
                l_ref[br, rows] = a * l_ref[br, rows] + jnp.sum(lrun, axis=-1, keepdims=True)
                acc_ref[br, rows] = a * acc_ref[br, rows] + jnp.dot(
                    p_ref[:, 0:kw], v_ref[0:kw, :], preferred_element_type=F32)
                m_ref[br, rows] = m_new

    @pl.when(ki < qi)
    def _():
        update(False)

    @pl.when(ki == qi)
    def _():
        update(True)
        lp = lamp_ref[...]
        lam = (jnp.exp(jnp.sum(lp[0:1] * lp[1:2], axis=-1, keepdims=True))
               - jnp.exp(jnp.sum(lp[2:3] * lp[3:4], axis=-1, keepdims=True)) + lam_init)
        o = acc_ref[0] / l_ref[0] - lam * (acc_ref[1] / l_ref[1])
        o = o * lax.rsqrt(jnp.mean(o * o, axis=-1, keepdims=True) + LN_EPS) * g_ref[...]
        o_ref[...] = (o * (1.0 - lam_init)).astype(o_ref.dtype)


def _diff_attention(q, k, v, lam_p, subln_g, batch, seq, lam_init):
    n, d = q.shape
    dv = 2 * HEAD_DIM
    n_heads = d // dv
    t = _pick(seq, ATTN_TILE)
    nq = seq // t
    pairs = [(a, b) for a in range(nq) for b in range(a + 1)]
    qi = jnp.asarray([p[0] for p in pairs], I32)
    ki = jnp.asarray([p[1] for p in pairs], I32)
    slopes = 2.0 ** (-8.0 * jnp.arange(1, n_heads + 1, dtype=F32) / n_heads)
    q3, k3, v3 = (a.reshape(batch, seq, d) for a in (q, k, v))
    rq, ck = _pick(t, ATTN_ROWS), _pick(t, ATTN_COLS)
    out = pl.pallas_call(
        functools.partial(_attn_kernel, tq=t, tk=t, rq=rq, ck=ck, lam_init=lam_init),
        out_shape=jax.ShapeDtypeStruct((batch, seq, d), BF16),
        grid_spec=pltpu.PrefetchScalarGridSpec(
            num_scalar_prefetch=3,
            grid=(batch, n_heads, len(pairs)),
            in_specs=[pl.BlockSpec((None, t, dv), lambda b, h, s, qi, ki, sl: (b, qi[s], h)),
                      pl.BlockSpec((None, t, dv), lambda b, h, s, qi, ki, sl: (b, ki[s], h)),
                      pl.BlockSpec((None, t, dv), lambda b, h, s, qi, ki, sl: (b, ki[s], h)),
                      pl.BlockSpec((4, HEAD_DIM), lambda b, h, s, qi, ki, sl: (0, 0)),
                      pl.BlockSpec((1, dv), lambda b, h, s, qi, ki, sl: (0, 0))],
            out_specs=pl.BlockSpec((None, t, dv), lambda b, h, s, qi, ki, sl: (b, qi[s], h)),
            scratch_shapes=[pltpu.VMEM((2, t, 1), F32), pltpu.VMEM((2, t, 1), F32),
                            pltpu.VMEM((2, t, dv), F32),
                            pltpu.VMEM((rq, t), F32), pltpu.VMEM((rq, t), BF16)]),
        compiler_params=_cp("parallel", "parallel", "arbitrary"),
        name="diff_attention",
    )(qi, ki, slopes, q3, k3, v3, lam_p.astype(F32), subln_g.reshape(1, dv).astype(F32))
    return out.reshape(n, d)


def kernel(x, c, ada_w, ada_b, ln_g, ln_b, conv_pw1_w, conv_pw1_b, conv_dw_w, conv_dw_b,
           conv_ln_g, conv_ln_b, conv_pw2_w, conv_pw2_b, attn_wk, attn_wv, attn_wq,
           attn_lambda, attn_subln_g, attn_wo, router_w, router_bias, exp_w_gate, exp_w_up,
           exp_w_down, shared_w_gate, shared_w_up, shared_w_down):
    batch, seq, d = x.shape
    depth = ada_w.shape[0]
    n_a = conv_pw1_w.shape[0]
    n = batch * seq
    alpha = (2 * depth) ** 0.25
    zeros_d = jnp.zeros((d,), F32)

    mod = _ada_mod(c, ada_w, ada_b)
    mods = [mod[l].reshape(batch, 1, 6 * d) for l in range(depth)]
    x2 = x.reshape(n, d)
    xb = None
    ub = _modulate(x2, mods[0], seq, 1, 0)
    kb = vb = None

    for l in range(depth):
        mod_l = mods[l]
        if l < n_a:
            h = _matmul_glu(ub, conv_pw1_w[l].astype(BF16), conv_pw1_b[l])
            h = _conv_ln_swish(h, conv_dw_w[l], conv_dw_b[l], conv_ln_g[l], conv_ln_b[l],
                               batch, seq)
            y = _matmul(h, conv_pw2_w[l].astype(BF16), conv_pw2_b[l])
        else:
            j = l - n_a
            if l == n_a:
                kb = _matmul(xb, attn_wk.astype(BF16), zeros_d)
                vb = _matmul(xb, attn_wv.astype(BF16), zeros_d)
            qb = _matmul(ub, attn_wq[j].astype(BF16), zeros_d, scale=HEAD_DIM ** -0.5 * LOG2E)
            lam_init = 0.8 - 0.6 * math.exp(-0.3 * l)
            o = _diff_attention(qb, kb, vb, attn_lambda[j], attn_subln_g[j], batch, seq, lam_init)
            y = _matmul(o, attn_wo[j].astype(BF16), zeros_d)
        x2, ub, up = _post(x2, y, mod_l, 2, ln_g[l, 0], ln_b[l, 0], mod_l, 4, 3, seq, alpha,
                           ("x", "ub", "up"))
        last = l == depth - 1
        kinds = ("x",) if last else (("x", "xb", "ub") if l + 1 >= n_a else ("x", "ub"))
        next_mod = mod_l if last else mods[l + 1]
        outs = _moe_sublayer(
            x2, ub, up, mod_l, ln_g[l, 1], ln_b[l, 1], next_mod, (1, 0), seq, alpha, kinds,
            router_w[l], router_bias[l], exp_w_gate[l].astype(BF16), exp_w_up[l].astype(BF16),
            exp_w_down[l].astype(BF16), shared_w_gate[l].astype(BF16),
            shared_w_up[l].astype(BF16), shared_w_down[l].astype(BF16))
        x2 = outs[0]
        if not last:
            ub = outs[-1]
            xb = outs[1] if len(outs) == 3 else None
    return x2.reshape(batch, seq, d)
```

```python
import functools
import math

import jax
import jax.numpy as jnp
import numpy as np
from jax import lax
from jax.experimental import pallas as pl
from jax.experimental.pallas import tpu as pltpu

F32 = jnp.float32
BF16 = jnp.bfloat16
U32 = jnp.uint32
I32 = jnp.int32

HEAD_DIM = 128
N_GROUPS = 8
TOPK_GROUPS = 4
TOP_K = 8
ROUTED_SCALE = 2.5
LN_EPS = 1e-5
LOG2E = math.log2(math.e)

LANES = 128
ATTN_TILE = 1024
ATTN_ROWS = 256
ATTN_COLS = 512
EXPERT_TILE = 256
ROUTER_TILE = 512
COMBINE_TILE = 128
VMEM_LIMIT = 56 * 1024 * 1024


def _cp(*sem):
    return pltpu.CompilerParams(dimension_semantics=sem, vmem_limit_bytes=VMEM_LIMIT)


def _pick(n, pref):
    t = min(n, pref)
    while n % t:
        t //= 2
    return t


def _bf16_bits(v):
    return lax.bitcast_convert_type(v.astype(BF16).astype(F32), U32)


def _store_packed(ref, val):
    w = ref.shape[-1]
    ref[...] = (_bf16_bits(val[:, :w]) >> 16) | _bf16_bits(val[:, w:])


def _load_packed(ref):
    w = ref[...]
    return (lax.bitcast_convert_type(w << 16, F32),
            lax.bitcast_convert_type(w & jnp.uint32(0xFFFF0000), F32))


def _ada_kernel(c_ref, w_ref, b_ref, o_ref, cs_ref, *, nb, tn):
    @pl.when((pl.program_id(0) == 0) & (pl.program_id(1) == 0))
    def _():
        c = c_ref[...]
        cs_ref[...] = c * jax.nn.sigmoid(c)

    for b in range(nb):
        parts = []
        for j in range(tn // LANES):
            prod = w_ref[:, j * LANES:(j + 1) * LANES] * cs_ref[b]
            parts.append(jnp.sum(prod, axis=0, keepdims=True))
        o_ref[b:b + 1, :] = jnp.concatenate(parts, axis=1) + b_ref[...]


def _ada_mod(c, ada_w, ada_b):
    nl, d, n6 = ada_w.shape
    nb = c.shape[0]
    tn = _pick(n6, 512)
    c_b = jnp.broadcast_to(c[:, :, None], (nb, d, LANES))
    return pl.pallas_call(
        functools.partial(_ada_kernel, nb=nb, tn=tn),
        out_shape=jax.ShapeDtypeStruct((nl, nb, n6), F32),
        grid=(nl, n6 // tn),
        in_specs=[
            pl.BlockSpec((nb, d, LANES), lambda l, j: (0, 0, 0)),
            pl.BlockSpec((None, d, tn), lambda l, j: (l, 0, j)),
            pl.BlockSpec((None, 1, tn), lambda l, j: (l, 0, j)),
        ],
        out_specs=pl.BlockSpec((None, nb, tn), lambda l, j: (l, 0, j)),
        scratch_shapes=[pltpu.VMEM((nb, d, LANES), F32)],
        compiler_params=_cp("arbitrary", "arbitrary"),
        name="ada_mod",
    )(c_b, ada_w, ada_b.reshape(nl, 1, n6))


def _layer_norm_rows(z, g, b):
    mu = jnp.mean(z, axis=-1, keepdims=True)
    zc = z - mu
    var = jnp.mean(zc * zc, axis=-1, keepdims=True)
    return zc * lax.rsqrt(var + LN_EPS) * g + b


def _mod_spec(d, tiles_per_batch, chunk):
    return pl.BlockSpec((None, 1, d), lambda i, _c=chunk, _t=tiles_per_batch: (i // _t, 0, _c))


def _modulate_kernel(x_ref, sc_ref, sh_ref, u_ref):
    u_ref[...] = (x_ref[...] * (1.0 + sc_ref[...]) + sh_ref[...]).astype(u_ref.dtype)


def _modulate(x2, mod_l, seq, sc_chunk, sh_chunk):
    n, d = x2.shape
    ts = _pick(seq, 512)
    tpb = seq // ts
    return pl.pallas_call(
        _modulate_kernel,
        out_shape=jax.ShapeDtypeStruct((n, d), BF16),
        grid=(n // ts,),
        in_specs=[pl.BlockSpec((ts, d), lambda i: (i, 0)),
                  _mod_spec(d, tpb, sc_chunk), _mod_spec(d, tpb, sh_chunk)],
        out_specs=pl.BlockSpec((ts, d), lambda i: (i, 0)),
        compiler_params=_cp("parallel"),
        name="modulate",
    )(x2, mod_l, mod_l)


def _mm_kernel(a_ref, w_ref, b_ref, o_ref, *, scale):
    acc = jnp.dot(a_ref[...], w_ref[...], preferred_element_type=F32) + b_ref[...]
    if scale != 1.0:
        acc = acc * scale
    o_ref[...] = acc.astype(o_ref.dtype)


def _matmul(a, w, bias, *, scale=1.0, out_dtype=BF16):
    m, k = a.shape
    n = w.shape[1]
    tm, tn = _pick(m, 1024), _pick(n, 1024)
    return pl.pallas_call(
        functools.partial(_mm_kernel, scale=scale),
        out_shape=jax.ShapeDtypeStruct((m, n), out_dtype),
        grid=(m // tm, n // tn),
        in_specs=[pl.BlockSpec((tm, k), lambda i, j: (i, 0)),
                  pl.BlockSpec((k, tn), lambda i, j: (0, j)),
                  pl.BlockSpec((1, tn), lambda i, j: (0, j))],
        out_specs=pl.BlockSpec((tm, tn), lambda i, j: (i, j)),
        compiler_params=_cp("parallel", "parallel"),
        name="matmul",
    )(a, w, bias.reshape(1, n))


def _glu_kernel(a_ref, wa_ref, wg_ref, ba_ref, bg_ref, o_ref):
    a = a_ref[...]
    lin = jnp.dot(a, wa_ref[...], preferred_element_type=F32) + ba_ref[...]
    gate = jnp.dot(a, wg_ref[...], preferred_element_type=F32) + bg_ref[...]
    o_ref[...] = (lin * jax.nn.sigmoid(gate)).astype(o_ref.dtype)


def _matmul_glu(a, w, bias):
    m, k = a.shape
    n = w.shape[1] // 2
    tm, tn = _pick(m, 1024), _pick(n, 512)
    nj = n // tn
    b2 = bias.reshape(1, 2 * n)
    return pl.pallas_call(
        _glu_kernel,
        out_shape=jax.ShapeDtypeStruct((m, n), BF16),
        grid=(m // tm, nj),
        in_specs=[pl.BlockSpec((tm, k), lambda i, j: (i, 0)),
                  pl.BlockSpec((k, tn), lambda i, j: (0, j)),
                  pl.BlockSpec((k, tn), lambda i, j, _nj=nj: (0, j + _nj)),
                  pl.BlockSpec((1, tn), lambda i, j: (0, j)),
                  pl.BlockSpec((1, tn), lambda i, j, _nj=nj: (0, j + _nj))],
        out_specs=pl.BlockSpec((tm, tn), lambda i, j: (i, j)),
        compiler_params=_cp("parallel", "parallel"),
        name="matmul_glu",
    )(a, w, w, b2, b2)


def _conv_kernel(cur_ref, halo_ref, w_ref, dwb_ref, g_ref, b_ref, o_ref, buf_ref, sh_ref, cv_ref,
                 *, ts, hb, kw, rc, cw):
    d = cur_ref.shape[-1]
    sub = 8
    first = pl.program_id(1) == 0
    buf_ref[0:hb, :] = jnp.where(first, 0.0, halo_ref[...].astype(F32))
    buf_ref[hb:hb + ts, :] = cur_ref[...].astype(F32)
    span = ts + hb - sub
    for c in range(d // cw):
        cols = slice(c * cw, (c + 1) * cw)
        for s in range(1, sub):
            sh_ref[s - 1] = buf_ref[s:s + span, cols]
        for r in range(ts // rc):
            acc = jnp.zeros((rc, cw), F32)
            for k in range(kw):
                off = hb - (kw - 1) + k + r * rc
                s, a = off % sub, off - off % sub
                win = buf_ref[a:a + rc, cols] if s == 0 else sh_ref[s - 1, a:a + rc, :]
                acc = acc + win * w_ref[k:k + 1, cols]
            cv_ref[r * rc:(r + 1) * rc, cols] = acc + dwb_ref[:, cols]
    y = _layer_norm_rows(cv_ref[...], g_ref[...], b_ref[...])
    o_ref[...] = (y * jax.nn.sigmoid(y)).astype(o_ref.dtype)


def _conv_ln_swish(h, dw_w, dw_b, g, b, batch, seq):
    n, d = h.shape
    kw = dw_w.shape[0]
    hb = 32
    assert kw - 1 <= hb and seq % hb == 0
    ts = _pick(seq, 256)
    rc, cw = _pick(ts, 64), _pick(d, 256)
    h3 = h.reshape(batch, seq, d)
    bpt = ts // hb
    out = pl.pallas_call(
        functools.partial(_conv_kernel, ts=ts, hb=hb, kw=kw, rc=rc, cw=cw),
        out_shape=jax.ShapeDtypeStruct((batch, seq, d), BF16),
        grid=(batch, seq // ts),
        in_specs=[pl.BlockSpec((None, ts, d), lambda bi, i: (bi, i, 0)),
                  pl.BlockSpec((None, hb, d),
                               lambda bi, i, _b=bpt: (bi, jnp.maximum(i * _b - 1, 0), 0)),
                  pl.BlockSpec((kw, d), lambda bi, i: (0, 0)),
                  pl.BlockSpec((1, d), lambda bi, i: (0, 0)),
                  pl.BlockSpec((1, d), lambda bi, i: (0, 0)),
                  pl.BlockSpec((1, d), lambda bi, i: (0, 0))],
        out_specs=pl.BlockSpec((None, ts, d), lambda bi, i: (bi, i, 0)),
        scratch_shapes=[pltpu.VMEM((hb + ts, d), F32), pltpu.VMEM((7, ts + hb - 8, cw), F32),
                        pltpu.VMEM((ts, d), F32)],
        compiler_params=_cp("parallel", "parallel"),
        name="dwconv_ln_swish",
    )(h3, h3, dw_w, dw_b.reshape(1, d), g.reshape(1, d), b.reshape(1, d))
    return out.reshape(n, d)


def _emit_outputs(xn, sc_ref, sh_ref, outs, kinds):
    u = None
    for ref, kind in zip(outs, kinds):
        if kind == "x":
            ref[...] = xn
        elif kind == "xb":
            ref[...] = xn.astype(BF16)
        else:
            if u is None:
                u = xn * (1.0 + sc_ref[...]) + sh_ref[...]
            if kind == "ub":
                ref[...] = u.astype(BF16)
            else:
                _store_packed(ref, u)


def _post_kernel(x_ref, y_ref, gate_ref, lng_ref, lnb_ref, sc_ref, sh_ref, *outs,
                 alpha, kinds):
    z = alpha * x_ref[...] + (1.0 + gate_ref[...]) * y_ref[...].astype(F32)
    xn = _layer_norm_rows(z, lng_ref[...], lnb_ref[...])
    _emit_outputs(xn, sc_ref, sh_ref, outs, kinds)


def _out_shapes_specs(kinds, n, d, ts):
    shapes, specs = [], []
    for kind in kinds:
        if kind == "x":
            shapes.append(jax.ShapeDtypeStruct((n, d), F32))
            specs.append(pl.BlockSpec((ts, d), lambda i, *_: (i, 0)))
        elif kind in ("xb", "ub"):
            shapes.append(jax.ShapeDtypeStruct((n, d), BF16))
            specs.append(pl.BlockSpec((ts, d), lambda i, *_: (i, 0)))
        else:
            shapes.append(jax.ShapeDtypeStruct((n, d // 2), U32))
            specs.append(pl.BlockSpec((ts, d // 2), lambda i, *_: (i, 0)))
    return shapes, specs


def _post(x2, y, mod_l, gate_chunk, lng, lnb, next_mod, sc_chunk, sh_chunk, seq, alpha, kinds):
    n, d = x2.shape
    ts = _pick(seq, 256)
    tpb = seq // ts
    shapes, specs = _out_shapes_specs(kinds, n, d, ts)
    return pl.pallas_call(
        functools.partial(_post_kernel, alpha=alpha, kinds=kinds),
        out_shape=shapes,
        grid=(n // ts,),
        in_specs=[pl.BlockSpec((ts, d), lambda i: (i, 0)),
                  pl.BlockSpec((ts, d), lambda i: (i, 0)),
                  _mod_spec(d, tpb, gate_chunk),
                  pl.BlockSpec((1, d), lambda i: (0, 0)),
                  pl.BlockSpec((1, d), lambda i: (0, 0)),
                  _mod_spec(d, tpb, sc_chunk), _mod_spec(d, tpb, sh_chunk)],
        out_specs=specs,
        compiler_params=_cp("parallel"),
        name="residual_ln",
    )(x2, y, mod_l, lng.reshape(1, d), lnb.reshape(1, d), next_mod, next_mod)


def _router_kernel(u_ref, rwt_ref, rb_ref, tri_ref, eid_ref, gate_ref, rk_ref, cnt_ref, run_ref,
                   *, n_exp, n_grp, topk_grp, top_k, scale):
    @pl.when(pl.program_id(0) == 0)
    def _():
        run_ref[...] = jnp.zeros_like(run_ref)

    logits = lax.dot_general(rwt_ref[...], u_ref[...], (((1,), (1,)), ((), ())),
                             preferred_element_type=F32)
    tm = logits.shape[1]
    gsz = n_exp // n_grp
    scores = jax.nn.sigmoid(logits)
    sel = scores + rb_ref[...]
    neg = jnp.float32(-jnp.inf)

    grp = sel.reshape(n_grp, gsz, tm)
    io = lax.broadcasted_iota(I32, (n_grp, gsz, tm), 1)
    m1 = jnp.max(grp, axis=1, keepdims=True)
    first = jnp.min(jnp.where(grp == m1, io, gsz), axis=1, keepdims=True)
    m2 = jnp.max(jnp.where(io == first, neg, grp), axis=1, keepdims=True)
    gs = (m1 + m2).reshape(n_grp, tm)

    gi = lax.broadcasted_iota(I32, (n_grp, tm), 0)
    grank = jnp.zeros((n_grp, tm), I32)
    for g2 in range(n_grp):
        row = gs[g2:g2 + 1, :]
        beats = (row > gs) | ((row == gs) & (gi > g2))
        grank = grank + beats.astype(I32)
    gmask = (grank < topk_grp).astype(F32)
    emask = jnp.broadcast_to(gmask.reshape(n_grp, 1, tm), (n_grp, gsz, tm)).reshape(n_exp, tm) > 0.5

    msel = jnp.where(emask, sel, neg)
    ei = lax.broadcasted_iota(I32, (n_exp, tm), 0)
    rank = jnp.zeros((n_exp, tm), I32)
    for e2 in range(n_exp):
        row = msel[e2:e2 + 1, :]
        beats = (row > msel) | ((row == msel) & (ei > e2))
        rank = rank + beats.astype(I32)
    chosen = (rank < top_k) & emask

    selw = jnp.where(chosen, scores, 0.0)
    denom = jnp.sum(selw, axis=0, keepdims=True)
    gates = selw / denom * scale

    chf = jnp.where(chosen, 1.0, 0.0)
    before = jnp.dot(chf.astype(BF16), tri_ref[...], preferred_element_type=F32) + run_ref[...]
    run_new = run_ref[...] + jnp.sum(chf, axis=1, keepdims=True)
    run_ref[...] = run_new
    cnt_ref[...] = jnp.broadcast_to(run_new, cnt_ref.shape)

    eif = ei.astype(F32)
    eids, gts, rks = [], [], []
    for k in range(top_k):
        mk = chosen & (rank == k)
        eids.append(jnp.sum(jnp.where(mk, eif, 0.0), axis=0, keepdims=True))
        gts.append(jnp.sum(jnp.where(mk, gates, 0.0), axis=0, keepdims=True))
        rks.append(jnp.sum(jnp.where(mk, before, 0.0), axis=0, keepdims=True))
    eid_ref[...] = jnp.concatenate(eids, axis=0).astype(I32)
    gate_ref[...] = jnp.concatenate(gts, axis=0)
    rk_ref[...] = jnp.concatenate(rks, axis=0).astype(I32)


def _router(ub, router_w, router_bias):
    n, d = ub.shape
    n_exp = router_w.shape[1]
    tm = _pick(n, ROUTER_TILE)
    tri = jnp.asarray(np.triu(np.ones((tm, tm), np.float32), k=1), BF16)
    outs = pl.pallas_call(
        functools.partial(_router_kernel, n_exp=n_exp, n_grp=N_GROUPS, topk_grp=TOPK_GROUPS,
                          top_k=TOP_K, scale=ROUTED_SCALE),
        out_shape=[jax.ShapeDtypeStruct((TOP_K, n), I32),
                   jax.ShapeDtypeStruct((TOP_K, n), F32),
                   jax.ShapeDtypeStruct((TOP_K, n), I32),
                   jax.ShapeDtypeStruct((n_exp, LANES), F32)],
        grid=(n // tm,),
        in_specs=[pl.BlockSpec((tm, d), lambda i: (i, 0)),
                  pl.BlockSpec((n_exp, d), lambda i: (0, 0)),
                  pl.BlockSpec((n_exp, 1), lambda i: (0, 0)),
                  pl.BlockSpec((tm, tm), lambda i: (0, 0))],
        out_specs=[pl.BlockSpec((TOP_K, tm), lambda i: (0, i)),
                   pl.BlockSpec((TOP_K, tm), lambda i: (0, i)),
                   pl.BlockSpec((TOP_K, tm), lambda i: (0, i)),
                   pl.BlockSpec((n_exp, LANES), lambda i: (0, 0))],
        scratch_shapes=[pltpu.VMEM((n_exp, 1), F32)],
        compiler_params=_cp("arbitrary"),
        name="moe_router",
    )(ub, router_w.T.astype(BF16), router_bias.reshape(n_exp, 1).astype(F32), tri)
    return outs


def _dispatch_kernel(pos_ref, up_ref, xg_ref, sem, *, tm, top_k):
    base = pl.program_id(0) * tm

    def copy(t, k):
        p = pos_ref[(base + t) * top_k + k]
        return pltpu.make_async_copy(up_ref.at[pl.ds(t, 1), :], xg_ref.at[pl.ds(p, 1), :], sem)

    def issue(t, carry):
        for k in range(top_k):
            copy(t, k).start()
        return carry

    def drain(t, carry):
        for k in range(top_k):
            copy(t, k).wait()
        return carry

    lax.fori_loop(0, tm, issue, 0)
    lax.fori_loop(0, tm, drain, 0)


def _dispatch(pos_flat, up, p_rows):
    n, w = up.shape
    tm = _pick(n, ROUTER_TILE)
    return pl.pallas_call(
        functools.partial(_dispatch_kernel, tm=tm, top_k=TOP_K),
        out_shape=jax.ShapeDtypeStruct((p_rows, w), U32),
        grid_spec=pltpu.PrefetchScalarGridSpec(
            num_scalar_prefetch=1,
            grid=(n // tm,),
            in_specs=[pl.BlockSpec((tm, w), lambda i, pos: (i, 0))],
            out_specs=pl.BlockSpec(memory_space=pl.ANY),
            scratch_shapes=[pltpu.SemaphoreType.DMA]),
        compiler_params=_cp("arbitrary"),
        name="moe_dispatch",
    )(pos_flat, up)


def _expert_kernel(te_ref, tv_ref, xg_ref, wg_ref, wu_ref, wd_ref, yg_ref, wgu_b, wd_b):
    j = pl.program_id(0)
    valid = tv_ref[j] != 0
    ff = wg_ref.shape[-1]

    @pl.when((j == 0) | (te_ref[j] != te_ref[jnp.maximum(j - 1, 0)]))
    def _():
        wgu_b[:, 0:ff] = wg_ref[...].astype(BF16)
        wgu_b[:, ff:2 * ff] = wu_ref[...].astype(BF16)
        wd_b[...] = wd_ref[...].astype(BF16)

    @pl.when(valid)
    def _():
        lo, hi = _load_packed(xg_ref)
        x = jnp.concatenate([lo, hi], axis=1).astype(BF16)
        gu = jnp.dot(x, wgu_b[...], preferred_element_type=F32)
        g, u = gu[:, 0:ff], gu[:, ff:2 * ff]
        h = (g * jax.nn.sigmoid(g) * u).astype(BF16)
        _store_packed(yg_ref, jnp.dot(h, wd_b[...], preferred_element_type=F32))

    @pl.when(jnp.logical_not(valid))
    def _():
        yg_ref[...] = jnp.zeros_like(yg_ref)


def _experts(tile_e, tile_v, xg, wg, wu, wd, layer, te_rows):
    n_tiles = tile_e.shape[0]
    _, _, d, ff = wg.shape
    w = xg.shape[1]
    return pl.pallas_call(
        _expert_kernel,
        out_shape=jax.ShapeDtypeStruct(xg.shape, U32),
        grid_spec=pltpu.PrefetchScalarGridSpec(
            num_scalar_prefetch=2,
            grid=(n_tiles,),
            in_specs=[pl.BlockSpec((te_rows, w), lambda j, te, tv: (j, 0)),
                      pl.BlockSpec((None, None, d, ff), lambda j, te, tv: (layer, te[j], 0, 0)),
                      pl.BlockSpec((None, None, d, ff), lambda j, te, tv: (layer, te[j], 0, 0)),
                      pl.BlockSpec((None, None, ff, d), lambda j, te, tv: (layer, te[j], 0, 0))],
            out_specs=pl.BlockSpec((te_rows, w), lambda j, te, tv: (j, 0)),
            scratch_shapes=[pltpu.VMEM((d, 2 * ff), BF16), pltpu.VMEM((ff, d), BF16)]),
        compiler_params=_cp("arbitrary"),
        name="moe_experts",
    )(tile_e, tile_v, xg, wg, wu, wd)


def _shared_kernel(u_ref, sg_ref, su_ref, sd_ref, o_ref):
    x = u_ref[...]
    g = jnp.dot(x, sg_ref[...], preferred_element_type=F32)
    u = jnp.dot(x, su_ref[...], preferred_element_type=F32)
    h = (g * jax.nn.sigmoid(g) * u).astype(BF16)
    o_ref[...] = jnp.dot(h, sd_ref[...], preferred_element_type=F32).astype(o_ref.dtype)


def _shared_expert(ub, sg, su, sd):
    n, d = ub.shape
    ff = sg.shape[1]
    tm = _pick(n, 512)
    return pl.pallas_call(
        _shared_kernel,
        out_shape=jax.ShapeDtypeStruct((n, d), BF16),
        grid=(n // tm,),
        in_specs=[pl.BlockSpec((tm, d), lambda i: (i, 0)),
                  pl.BlockSpec((d, ff), lambda i: (0, 0)),
                  pl.BlockSpec((d, ff), lambda i: (0, 0)),
                  pl.BlockSpec((ff, d), lambda i: (0, 0))],
        out_specs=pl.BlockSpec((tm, d), lambda i: (i, 0)),
        compiler_params=_cp("parallel"),
        name="moe_shared",
    )(ub, sg, su, sd)


def _combine_kernel(pos_ref, yg_ref, gw_ref, ysh_ref, x_ref, gate_ref, lng_ref, lnb_ref,
                    sc_ref, sh_ref, *rest, tc, top_k, alpha, kinds):
    outs, (gbuf, ybuf, sem) = rest[:len(kinds)], rest[len(kinds):]
    base = pl.program_id(0) * tc
    half = gbuf.shape[-1]

    def copy(t, k):
        p = pos_ref[(base + t) * top_k + k]
        return pltpu.make_async_copy(yg_ref.at[pl.ds(p, 1), :], gbuf.at[k, pl.ds(t, 1), :], sem)

    def issue(t, carry):
        for k in range(top_k):
            copy(t, k).start()
        return carry

    def drain(t, carry):
        for k in range(top_k):
            copy(t, k).wait()
        return carry

    lax.fori_loop(0, tc, issue, 0)
    lax.fori_loop(0, tc, drain, 0)

    rc, cw = _pick(tc, 64), _pick(half, 256)
    for r in range(tc // rc):
        rows = slice(r * rc, (r + 1) * rc)
        gw = gw_ref[rows, :]
        for c in range(half // cw):
            cols = slice(c * cw, (c + 1) * cw)
            acc_lo = acc_hi = None
            for k in range(top_k):
                lo, hi = _load_packed(gbuf.at[k, rows, cols])
                wk = gw[:, k:k + 1]
                acc_lo = lo * wk if k == 0 else acc_lo + lo * wk
                acc_hi = hi * wk if k == 0 else acc_hi + hi * wk
            ybuf[rows, cols] = acc_lo
            ybuf[rows, half + c * cw:half + (c + 1) * cw] = acc_hi
    y = ybuf[...] + ysh_ref[...].astype(F32)
    z = alpha * x_ref[...] + (1.0 + gate_ref[...]) * y
    xn = _layer_norm_rows(z, lng_ref[...], lnb_ref[...])
    _emit_outputs(xn, sc_ref, sh_ref, outs, kinds)


def _combine(pos_flat, yg, gate_t, ysh, x2, mod_l, gate_chunk, lng, lnb, next_mod, sc_chunk,
             sh_chunk, seq, alpha, kinds):
    n, d = x2.shape
    tc = _pick(seq, COMBINE_TILE)
    tpb = seq // tc
    shapes, specs = _out_shapes_specs(kinds, n, d, tc)

    def mspec(chunk):
        return pl.BlockSpec((None, 1, d), lambda i, pos, _c=chunk: (i // tpb, 0, _c))

    return pl.pallas_call(
        functools.partial(_combine_kernel, tc=tc, top_k=TOP_K, alpha=alpha, kinds=kinds),
        out_shape=shapes,
        grid_spec=pltpu.PrefetchScalarGridSpec(
            num_scalar_prefetch=1,
            grid=(n // tc,),
            in_specs=[pl.BlockSpec(memory_space=pl.ANY),
                      pl.BlockSpec((tc, TOP_K), lambda i, pos: (i, 0)),
                      pl.BlockSpec((tc, d), lambda i, pos: (i, 0)),
                      pl.BlockSpec((tc, d), lambda i, pos: (i, 0)),
                      mspec(gate_chunk),
                      pl.BlockSpec((1, d), lambda i, pos: (0, 0)),
                      pl.BlockSpec((1, d), lambda i, pos: (0, 0)),
                      mspec(sc_chunk), mspec(sh_chunk)],
            out_specs=specs,
            scratch_shapes=[pltpu.VMEM((TOP_K, tc, d // 2), U32), pltpu.VMEM((tc, d), F32),
                            pltpu.SemaphoreType.DMA]),
        compiler_params=_cp("arbitrary"),
        name="moe_combine",
    )(pos_flat, yg, gate_t, ysh, x2, mod_l, lng.reshape(1, d), lnb.reshape(1, d),
      next_mod, next_mod)


def _moe_sublayer(x2, ub, up, mod_l, lng, lnb, next_mod, next_chunks, seq, alpha, kinds,
                  router_w, router_bias, wg, wu, wd, layer, sg, su, sd):
    n, d = x2.shape
    n_exp = router_w.shape[1]
    te_rows = _pick(n * TOP_K // n_exp, EXPERT_TILE)
    p_rows = n * TOP_K + n_exp * te_rows

    eid, gate, rk, cnt = _router(ub, router_w, router_bias)
    counts = cnt[:, 0].astype(I32)
    padded = (counts + te_rows - 1) // te_rows * te_rows
    ends = jnp.cumsum(padded)
    offs = ends - padded
    onehot = eid[:, :, None] == jnp.arange(n_exp, dtype=I32)
    pos_flat = (jnp.sum(jnp.where(onehot, offs, 0), axis=-1) + rk).T.reshape(-1)
    tile_start = jnp.arange(p_rows // te_rows, dtype=I32) * te_rows
    tile_e = jnp.minimum(jnp.sum((ends[None, :] <= tile_start[:, None]).astype(I32), axis=1),
                         n_exp - 1)
    tile_v = (tile_start < ends[-1]).astype(I32)

    xg = _dispatch(pos_flat, up, p_rows)
    yg = _experts(tile_e, tile_v, xg, wg, wu, wd, layer, te_rows)
    ysh = _shared_expert(ub, sg, su, sd)
    return _combine(pos_flat, yg, gate.T, ysh, x2, mod_l, 5, lng, lnb, next_mod,
                    next_chunks[0], next_chunks[1], seq, alpha, kinds)


def _attn_kernel(qi_ref, ki_ref, slope_ref, q_ref, k_ref, v_ref, lamp_ref, g_ref, o_ref,
                 m_ref, l_ref, acc_ref, s_ref, p_ref, *, tq, tk, rq, ck, lam_init):
    step = pl.program_id(2)
    qi = qi_ref[step]
    ki = ki_ref[step]
    slope = slope_ref[pl.program_id(1)]

    @pl.when(ki == 0)
    def _():
        m_ref[...] = jnp.full_like(m_ref, -jnp.inf)
        l_ref[...] = jnp.zeros_like(l_ref)
        acc_ref[...] = jnp.zeros_like(acc_ref)

    col = lax.broadcasted_iota(I32, (1, tk), 1)
    colbias = (slope * LOG2E) * (ki * tk - qi * tq + col).astype(F32)

    def fold(x, op):
        out = x[:, 0:LANES]
        for j in range(1, x.shape[1] // LANES):
            out = op(out, x[:, j * LANES:(j + 1) * LANES])
        return out

    def update(masked):
        for rb in range(tq // rq):
            rows = slice(rb * rq, (rb + 1) * rq)
            n_ck = ((rb + 1) * rq + ck - 1) // ck if masked else tk // ck
            kw = n_ck * ck
            for br in range(2):
                cols = slice(br * HEAD_DIM, (br + 1) * HEAD_DIM)
                q_blk = q_ref[rows, cols]
                mrun = None
                for cc in range(n_ck):
                    kcols = slice(cc * ck, (cc + 1) * ck)
                    s = lax.dot_general(q_blk, k_ref[kcols, cols], (((1,), (1,)), ((), ())),
                                        preferred_element_type=F32) + colbias[:, kcols]
                    if masked and (cc + 1) * ck > rb * rq + 1:
                        keep = (lax.broadcasted_iota(I32, (rq, ck), 0) + rb * rq
                                >= lax.broadcasted_iota(I32, (rq, ck), 1) + cc * ck)
                        s = jnp.where(keep, s, -jnp.inf)
                    s_ref[:, kcols] = s
                    part = fold(s, jnp.maximum)
                    mrun = part if mrun is None else jnp.maximum(mrun, part)
                m_prev = m_ref[br, rows]
                m_new = jnp.maximum(m_prev, jnp.max(mrun, axis=-1, keepdims=True))
                a = jnp.exp2(m_prev - m_new)
                lrun = None
                for cc in range(n_ck):
                    kcols = slice(cc * ck, (cc + 1) * ck)
                    p = jnp.exp2(s_ref[:, kcols] - m_new)
                    p_ref[:, kcols] = p.astype(BF16)
                    part = fold(p, jnp.add)
                    lrun = part if lrun is None else lrun + part
                l_ref[br, rows] = a * l_ref[br, rows] + jnp.sum(lrun, axis=-1, keepdims=True)
                acc_ref[br, rows] = a * acc_ref[br, rows] + jnp.dot(
                    p_ref[:, 0:kw], v_ref[0:kw, :], preferred_element_type=F32)
                m_ref[br, rows] = m_new

    @pl.when(ki < qi)
    def _():
        update(False)

    @pl.when(ki == qi)
    def _():
        update(True)
        lp = lamp_ref[...]
        lam = (jnp.exp(jnp.sum(lp[0:1] * lp[1:2], axis=-1, keepdims=True))
               - jnp.exp(jnp.sum(lp[2:3] * lp[3:4], axis=-1, keepdims=True)) + lam_init)
        o = acc_ref[0] / l_ref[0] - lam * (acc_ref[1] / l_ref[1])
        o = o * lax.rsqrt(jnp.mean(o * o, axis=-1, keepdims=True) + LN_EPS) * g_ref[...]
        o_ref[...] = (o * (1.0 - lam_init)).astype(o_ref.dtype)


def _diff_attention(q, k, v, lam_p, subln_g, batch, seq, lam_init):
    n, d = q.shape
    dv = 2 * HEAD_DIM
    n_heads = d // dv
    t = _pick(seq, ATTN_TILE)
    nq = seq // t
    pairs = [(a, b) for a in range(nq) for b in range(a + 1)]
    qi = jnp.asarray([p[0] for p in pairs], I32)
    ki = jnp.asarray([p[1] for p in pairs], I32)
    slopes = 2.0 ** (-8.0 * jnp.arange(1, n_heads + 1, dtype=F32) / n_heads)
    q3, k3, v3 = (a.reshape(batch, seq, d) for a in (q, k, v))
    rq, ck = _pick(t, ATTN_ROWS), _pick(t, ATTN_COLS)
    out = pl.pallas_call(
        functools.partial(_attn_kernel, tq=t, tk=t, rq=rq, ck=ck, lam_init=lam_init),
        out_shape=jax.ShapeDtypeStruct((batch, seq, d), BF16),
        grid_spec=pltpu.PrefetchScalarGridSpec(
            num_scalar_prefetch=3,
            grid=(batch, n_heads, len(pairs)),
            in_specs=[pl.BlockSpec((None, t, dv), lambda b, h, s, qi, ki, sl: (b, qi[s], h)),
                      pl.BlockSpec((None, t, dv), lambda b, h, s, qi, ki, sl: (b, ki[s], h)),
                      pl.BlockSpec((None, t, dv), lambda b, h, s, qi, ki, sl: (b, ki[s], h)),
                      pl.BlockSpec((4, HEAD_DIM), lambda b, h, s, qi, ki, sl: (0, 0)),
                      pl.BlockSpec((1, dv), lambda b, h, s, qi, ki, sl: (0, 0))],
            out_specs=pl.BlockSpec((None, t, dv), lambda b, h, s, qi, ki, sl: (b, qi[s], h)),
            scratch_shapes=[pltpu.VMEM((2, t, 1), F32), pltpu.VMEM((2, t, 1), F32),
                            pltpu.VMEM((2, t, dv), F32),
                            pltpu.VMEM((rq, t), F32), pltpu.VMEM((rq, t), BF16)]),
        compiler_params=_cp("parallel", "parallel", "arbitrary"),
        name="diff_attention",
    )(qi, ki, slopes, q3, k3, v3, lam_p.astype(F32), subln_g.reshape(1, dv).astype(F32))
    return out.reshape(n, d)


def kernel(x, c, ada_w, ada_b, ln_g, ln_b, conv_pw1_w, conv_pw1_b, conv_dw_w, conv_dw_b,
           conv_ln_g, conv_ln_b, conv_pw2_w, conv_pw2_b, attn_wk, attn_wv, attn_wq,
           attn_lambda, attn_subln_g, attn_wo, router_w, router_bias, exp_w_gate, exp_w_up,
           exp_w_down, shared_w_gate, shared_w_up, shared_w_down):
    batch, seq, d = x.shape
    depth = ada_w.shape[0]
    n_a = conv_pw1_w.shape[0]
    n = batch * seq
    alpha = (2 * depth) ** 0.25
    zeros_d = jnp.zeros((d,), F32)

    mod = _ada_mod(c, ada_w, ada_b)
    mods = [mod[l].reshape(batch, 1, 6 * d) for l in range(depth)]
    x2 = x.reshape(n, d)
    xb = None
    ub = _modulate(x2, mods[0], seq, 1, 0)
    kb = vb = None

    for l in range(depth):
        mod_l = mods[l]
        if l < n_a:
            h = _matmul_glu(ub, conv_pw1_w[l].astype(BF16), conv_pw1_b[l])
            h = _conv_ln_swish(h, conv_dw_w[l], conv_dw_b[l], conv_ln_g[l], conv_ln_b[l],
                               batch, seq)
            y = _matmul(h, conv_pw2_w[l].astype(BF16), conv_pw2_b[l])
        else:
            j = l - n_a
            if l == n_a:
                kb = _matmul(xb, attn_wk.astype(BF16), zeros_d)
                vb = _matmul(xb, attn_wv.astype(BF16), zeros_d)
            qb = _matmul(ub, attn_wq[j].astype(BF16), zeros_d, scale=HEAD_DIM ** -0.5 * LOG2E)
            lam_init = 0.8 - 0.6 * math.exp(-0.3 * l)
            o = _diff_attention(qb, kb, vb, attn_lambda[j], attn_subln_g[j], batch, seq, lam_init)
            y = _matmul(o, attn_wo[j].astype(BF16), zeros_d)
        x2, ub, up = _post(x2, y, mod_l, 2, ln_g[l, 0], ln_b[l, 0], mod_l, 4, 3, seq, alpha,
                           ("x", "ub", "up"))
        last = l == depth - 1
        kinds = ("x",) if last else (("x", "xb", "ub") if l + 1 >= n_a else ("x", "ub"))
        next_mod = mod_l if last else mods[l + 1]
        outs = _moe_sublayer(
            x2, ub, up, mod_l, ln_g[l, 1], ln_b[l, 1], next_mod, (1, 0), seq, alpha, kinds,
            router_w[l], router_bias[l], exp_w_gate, exp_w_up, exp_w_down, l,
            shared_w_gate[l].astype(BF16),
            shared_w_up[l].astype(BF16), shared_w_down[l].astype(BF16))
        x2 = outs[0]
        if not last:
            ub = outs[-1]
            xb = outs[1] if len(outs) == 3 else None
    return x2.reshape(batch, seq, d)
```

```python
import functools
import math

import jax
import jax.numpy as jnp
import numpy as np
from jax import lax
from jax.experimental import pallas as pl
from jax.experimental.pallas import tpu as pltpu

F32 = jnp.float32
BF16 = jnp.bfloat16
U32 = jnp.uint32
I32 = jnp.int32

HEAD_DIM = 128
N_GROUPS = 8
TOPK_GROUPS = 4
TOP_K = 8
ROUTED_SCALE = 2.5
LN_EPS = 1e-5
LOG2E = math.log2(math.e)

LANES = 128
ATTN_TILE = 1024
ATTN_ROWS = 256
ATTN_COLS = 512
EXPERT_TILE = 256
ROUTER_TILE = 512
COMBINE_TILE = 128
VMEM_LIMIT = 56 * 1024 * 1024


def _cp(*sem):
    return pltpu.CompilerParams(dimension_semantics=sem, vmem_limit_bytes=VMEM_LIMIT)


def _pick(n, pref):
    t = min(n, pref)
    while n % t:
        t //= 2
    return t


def _bf16_bits(v):
    return lax.bitcast_convert_type(v.astype(BF16).astype(F32), U32)


def _store_packed(ref, val):
    w = ref.shape[-1]
    ref[...] = (_bf16_bits(val[:, :w]) >> 16) | _bf16_bits(val[:, w:])


def _load_packed(ref):
    w = ref[...]
    return (lax.bitcast_convert_type(w << 16, F32),
            lax.bitcast_convert_type(w & jnp.uint32(0xFFFF0000), F32))


def _ada_kernel(c_ref, w_ref, b_ref, o_ref, cs_ref, *, nb, tn):
    @pl.when((pl.program_id(0) == 0) & (pl.program_id(1) == 0))
    def _():
        c = c_ref[...]
        cs_ref[...] = c * jax.nn.sigmoid(c)

    for b in range(nb):
        parts = []
        for j in range(tn // LANES):
            prod = w_ref[:, j * LANES:(j + 1) * LANES] * cs_ref[b]
            parts.append(jnp.sum(prod, axis=0, keepdims=True))
        o_ref[b:b + 1, :] = jnp.concatenate(parts, axis=1) + b_ref[...]


def _ada_mod(c, ada_w, ada_b):
    nl, d, n6 = ada_w.shape
    nb = c.shape[0]
    tn = _pick(n6, 512)
    c_b = jnp.broadcast_to(c[:, :, None], (nb, d, LANES))
    return pl.pallas_call(
        functools.partial(_ada_kernel, nb=nb, tn=tn),
        out_shape=jax.ShapeDtypeStruct((nl, nb, n6), F32),
        grid=(nl, n6 // tn),
        in_specs=[
            pl.BlockSpec((nb, d, LANES), lambda l, j: (0, 0, 0)),
            pl.BlockSpec((None, d, tn), lambda l, j: (l, 0, j)),
            pl.BlockSpec((None, 1, tn), lambda l, j: (l, 0, j)),
        ],
        out_specs=pl.BlockSpec((None, nb, tn), lambda l, j: (l, 0, j)),
        scratch_shapes=[pltpu.VMEM((nb, d, LANES), F32)],
        compiler_params=_cp("arbitrary", "arbitrary"),
        name="ada_mod",
    )(c_b, ada_w, ada_b.reshape(nl, 1, n6))


def _layer_norm_rows(z, g, b):
    mu = jnp.mean(z, axis=-1, keepdims=True)
    zc = z - mu
    var = jnp.mean(zc * zc, axis=-1, keepdims=True)
    return zc * lax.rsqrt(var + LN_EPS) * g + b


def _mod_spec(d, tiles_per_batch, chunk):
    return pl.BlockSpec((None, 1, d), lambda i, _c=chunk, _t=tiles_per_batch: (i // _t, 0, _c))


def _modulate_kernel(x_ref, sc_ref, sh_ref, u_ref):
    u_ref[...] = (x_ref[...] * (1.0 + sc_ref[...]) + sh_ref[...]).astype(u_ref.dtype)


def _modulate(x2, mod_l, seq, sc_chunk, sh_chunk):
    n, d = x2.shape
    ts = _pick(seq, 512)
    tpb = seq // ts
    return pl.pallas_call(
        _modulate_kernel,
        out_shape=jax.ShapeDtypeStruct((n, d), BF16),
        grid=(n // ts,),
        in_specs=[pl.BlockSpec((ts, d), lambda i: (i, 0)),
                  _mod_spec(d, tpb, sc_chunk), _mod_spec(d, tpb, sh_chunk)],
        out_specs=pl.BlockSpec((ts, d), lambda i: (i, 0)),
        compiler_params=_cp("parallel"),
        name="modulate",
    )(x2, mod_l, mod_l)


def _mm_kernel(a_ref, w_ref, b_ref, o_ref, *, scale):
    acc = jnp.dot(a_ref[...], w_ref[...], preferred_element_type=F32) + b_ref[...]
    if scale != 1.0:
        acc = acc * scale
    o_ref[...] = acc.astype(o_ref.dtype)


def _matmul(a, w, bias, *, scale=1.0, out_dtype=BF16):
    m, k = a.shape
    n = w.shape[1]
    tm, tn = _pick(m, 1024), _pick(n, 1024)
    return pl.pallas_call(
        functools.partial(_mm_kernel, scale=scale),
        out_shape=jax.ShapeDtypeStruct((m, n), out_dtype),
        grid=(m // tm, n // tn),
        in_specs=[pl.BlockSpec((tm, k), lambda i, j: (i, 0)),
                  pl.BlockSpec((k, tn), lambda i, j: (0, j)),
                  pl.BlockSpec((1, tn), lambda i, j: (0, j))],
        out_specs=pl.BlockSpec((tm, tn), lambda i, j: (i, j)),
        compiler_params=_cp("parallel", "parallel"),
        name="matmul",
    )(a, w, bias.reshape(1, n))


def _glu_kernel(a_ref, wa_ref, wg_ref, ba_ref, bg_ref, o_ref):
    a = a_ref[...]
    lin = jnp.dot(a, wa_ref[...], preferred_element_type=F32) + ba_ref[...]
    gate = jnp.dot(a, wg_ref[...], preferred_element_type=F32) + bg_ref[...]
    o_ref[...] = (lin * jax.nn.sigmoid(gate)).astype(o_ref.dtype)


def _matmul_glu(a, w, bias):
    m, k = a.shape
    n = w.shape[1] // 2
    tm, tn = _pick(m, 1024), _pick(n, 512)
    nj = n // tn
    b2 = bias.reshape(1, 2 * n)
    return pl.pallas_call(
        _glu_kernel,
        out_shape=jax.ShapeDtypeStruct((m, n), BF16),
        grid=(m // tm, nj),
        in_specs=[pl.BlockSpec((tm, k), lambda i, j: (i, 0)),
                  pl.BlockSpec((k, tn), lambda i, j: (0, j)),
                  pl.BlockSpec((k, tn), lambda i, j, _nj=nj: (0, j + _nj)),
                  pl.BlockSpec((1, tn), lambda i, j: (0, j)),
                  pl.BlockSpec((1, tn), lambda i, j, _nj=nj: (0, j + _nj))],
        out_specs=pl.BlockSpec((tm, tn), lambda i, j: (i, j)),
        compiler_params=_cp("parallel", "parallel"),
        name="matmul_glu",
    )(a, w, w, b2, b2)


def _conv_kernel(cur_ref, halo_ref, w_ref, dwb_ref, g_ref, b_ref, o_ref, buf_ref, sh_ref, cv_ref,
                 *, ts, hb, kw, rc, cw):
    d = cur_ref.shape[-1]
    sub = 8
    first = pl.program_id(1) == 0
    buf_ref[0:hb, :] = jnp.where(first, 0.0, halo_ref[...].astype(F32))
    buf_ref[hb:hb + ts, :] = cur_ref[...].astype(F32)
    span = ts + hb - sub
    for c in range(d // cw):
        cols = slice(c * cw, (c + 1) * cw)
        for s in range(1, sub):
            sh_ref[s - 1] = buf_ref[s:s + span, cols]
        for r in range(ts // rc):
            acc = jnp.zeros((rc, cw), F32)
            for k in range(kw):
                off = hb - (kw - 1) + k + r * rc
                s, a = off % sub, off - off % sub
                win = buf_ref[a:a + rc, cols] if s == 0 else sh_ref[s - 1, a:a + rc, :]
                acc = acc + win * w_ref[k:k + 1, cols]
            cv_ref[r * rc:(r + 1) * rc, cols] = acc + dwb_ref[:, cols]
    y = _layer_norm_rows(cv_ref[...], g_ref[...], b_ref[...])
    o_ref[...] = (y * jax.nn.sigmoid(y)).astype(o_ref.dtype)


def _conv_ln_swish(h, dw_w, dw_b, g, b, batch, seq):
    n, d = h.shape
    kw = dw_w.shape[0]
    hb = 32
    assert kw - 1 <= hb and seq % hb == 0
    ts = _pick(seq, 256)
    rc, cw = _pick(ts, 64), _pick(d, 256)
    h3 = h.reshape(batch, seq, d)
    bpt = ts // hb
    out = pl.pallas_call(
        functools.partial(_conv_kernel, ts=ts, hb=hb, kw=kw, rc=rc, cw=cw),
        out_shape=jax.ShapeDtypeStruct((batch, seq, d), BF16),
        grid=(batch, seq // ts),
        in_specs=[pl.BlockSpec((None, ts, d), lambda bi, i: (bi, i, 0)),
                  pl.BlockSpec((None, hb, d),
                               lambda bi, i, _b=bpt: (bi, jnp.maximum(i * _b - 1, 0), 0)),
                  pl.BlockSpec((kw, d), lambda bi, i: (0, 0)),
                  pl.BlockSpec((1, d), lambda bi, i: (0, 0)),
                  pl.BlockSpec((1, d), lambda bi, i: (0, 0)),
                  pl.BlockSpec((1, d), lambda bi, i: (0, 0))],
        out_specs=pl.BlockSpec((None, ts, d), lambda bi, i: (bi, i, 0)),
        scratch_shapes=[pltpu.VMEM((hb + ts, d), F32), pltpu.VMEM((7, ts + hb - 8, cw), F32),
                        pltpu.VMEM((ts, d), F32)],
        compiler_params=_cp("parallel", "parallel"),
        name="dwconv_ln_swish",
    )(h3, h3, dw_w, dw_b.reshape(1, d), g.reshape(1, d), b.reshape(1, d))
    return out.reshape(n, d)


def _emit_outputs(xn, sc_ref, sh_ref, outs, kinds):
    u = None
    for ref, kind in zip(outs, kinds):
        if kind == "x":
            ref[...] = xn
        elif kind == "xb":
            ref[...] = xn.astype(BF16)
        else:
            if u is None:
                u = xn * (1.0 + sc_ref[...]) + sh_ref[...]
            if kind == "ub":
                ref[...] = u.astype(BF16)
            else:
                _store_packed(ref, u)


def _post_kernel(x_ref, y_ref, gate_ref, lng_ref, lnb_ref, sc_ref, sh_ref, *outs,
                 alpha, kinds):
    z = alpha * x_ref[...].astype(F32) + (1.0 + gate_ref[...]) * y_ref[...].astype(F32)
    xn = _layer_norm_rows(z, lng_ref[...], lnb_ref[...])
    _emit_outputs(xn, sc_ref, sh_ref, outs, kinds)


def _out_shapes_specs(kinds, n, d, ts):
    shapes, specs = [], []
    for kind in kinds:
        if kind == "x":
            shapes.append(jax.ShapeDtypeStruct((n, d), F32))
            specs.append(pl.BlockSpec((ts, d), lambda i, *_: (i, 0)))
        elif kind in ("xb", "ub"):
            shapes.append(jax.ShapeDtypeStruct((n, d), BF16))
            specs.append(pl.BlockSpec((ts, d), lambda i, *_: (i, 0)))
        else:
            shapes.append(jax.ShapeDtypeStruct((n, d // 2), U32))
            specs.append(pl.BlockSpec((ts, d // 2), lambda i, *_: (i, 0)))
    return shapes, specs


def _post(x2, y, mod_l, gate_chunk, lng, lnb, next_mod, sc_chunk, sh_chunk, seq, alpha, kinds):
    n, d = x2.shape
    ts = _pick(seq, 256)
    tpb = seq // ts
    shapes, specs = _out_shapes_specs(kinds, n, d, ts)
    return pl.pallas_call(
        functools.partial(_post_kernel, alpha=alpha, kinds=kinds),
        out_shape=shapes,
        grid=(n // ts,),
        in_specs=[pl.BlockSpec((ts, d), lambda i: (i, 0)),
                  pl.BlockSpec((ts, d), lambda i: (i, 0)),
                  _mod_spec(d, tpb, gate_chunk),
                  pl.BlockSpec((1, d), lambda i: (0, 0)),
                  pl.BlockSpec((1, d), lambda i: (0, 0)),
                  _mod_spec(d, tpb, sc_chunk), _mod_spec(d, tpb, sh_chunk)],
        out_specs=specs,
        compiler_params=_cp("parallel"),
        name="residual_ln",
    )(x2, y, mod_l, lng.reshape(1, d), lnb.reshape(1, d), next_mod, next_mod)


def _router_kernel(u_ref, rwt_ref, rb_ref, tri_ref, eid_ref, gate_ref, rk_ref, cnt_ref, run_ref,
                   *, n_exp, n_grp, topk_grp, top_k, scale):
    @pl.when(pl.program_id(0) == 0)
    def _():
        run_ref[...] = jnp.zeros_like(run_ref)

    logits = lax.dot_general(rwt_ref[...], u_ref[...], (((1,), (1,)), ((), ())),
                             preferred_element_type=F32)
    tm = logits.shape[1]
    gsz = n_exp // n_grp
    scores = jax.nn.sigmoid(logits)
    sel = scores + rb_ref[...]
    neg = jnp.float32(-jnp.inf)

    grp = sel.reshape(n_grp, gsz, tm)
    io = lax.broadcasted_iota(I32, (n_grp, gsz, tm), 1)
    m1 = jnp.max(grp, axis=1, keepdims=True)
    first = jnp.min(jnp.where(grp == m1, io, gsz), axis=1, keepdims=True)
    m2 = jnp.max(jnp.where(io == first, neg, grp), axis=1, keepdims=True)
    gs = (m1 + m2).reshape(n_grp, tm)

    gi = lax.broadcasted_iota(I32, (n_grp, tm), 0)
    grank = jnp.zeros((n_grp, tm), I32)
    for g2 in range(n_grp):
        row = gs[g2:g2 + 1, :]
        beats = (row > gs) | ((row == gs) & (gi > g2))
        grank = grank + beats.astype(I32)
    gmask = (grank < topk_grp).astype(F32)
    emask = jnp.broadcast_to(gmask.reshape(n_grp, 1, tm), (n_grp, gsz, tm)).reshape(n_exp, tm) > 0.5

    msel = jnp.where(emask, sel, neg)
    ei = lax.broadcasted_iota(I32, (n_exp, tm), 0)
    rank = jnp.zeros((n_exp, tm), I32)
    for e2 in range(n_exp):
        row = msel[e2:e2 + 1, :]
        beats = (row > msel) | ((row == msel) & (ei > e2))
        rank = rank + beats.astype(I32)
    chosen = (rank < top_k) & emask

    selw = jnp.where(chosen, scores, 0.0)
    denom = jnp.sum(selw, axis=0, keepdims=True)
    gates = selw / denom * scale

    chf = jnp.where(chosen, 1.0, 0.0)
    before = jnp.dot(chf.astype(BF16), tri_ref[...], preferred_element_type=F32) + run_ref[...]
    run_new = run_ref[...] + jnp.sum(chf, axis=1, keepdims=True)
    run_ref[...] = run_new
    cnt_ref[...] = jnp.broadcast_to(run_new, cnt_ref.shape)

    eif = ei.astype(F32)
    eids, gts, rks = [], [], []
    for k in range(top_k):
        mk = chosen & (rank == k)
        eids.append(jnp.sum(jnp.where(mk, eif, 0.0), axis=0, keepdims=True))
        gts.append(jnp.sum(jnp.where(mk, gates, 0.0), axis=0, keepdims=True))
        rks.append(jnp.sum(jnp.where(mk, before, 0.0), axis=0, keepdims=True))
    eid_ref[...] = jnp.concatenate(eids, axis=0).astype(I32)
    gate_ref[...] = jnp.concatenate(gts, axis=0)
    rk_ref[...] = jnp.concatenate(rks, axis=0).astype(I32)


def _router(ub, router_w, router_bias):
    n, d = ub.shape
    n_exp = router_w.shape[1]
    tm = _pick(n, ROUTER_TILE)
    tri = jnp.asarray(np.triu(np.ones((tm, tm), np.float32), k=1), BF16)
    outs = pl.pallas_call(
        functools.partial(_router_kernel, n_exp=n_exp, n_grp=N_GROUPS, topk_grp=TOPK_GROUPS,
                          top_k=TOP_K, scale=ROUTED_SCALE),
        out_shape=[jax.ShapeDtypeStruct((TOP_K, n), I32),
                   jax.ShapeDtypeStruct((TOP_K, n), F32),
                   jax.ShapeDtypeStruct((TOP_K, n), I32),
                   jax.ShapeDtypeStruct((n_exp, LANES), F32)],
        grid=(n // tm,),
        in_specs=[pl.BlockSpec((tm, d), lambda i: (i, 0)),
                  pl.BlockSpec((n_exp, d), lambda i: (0, 0)),
                  pl.BlockSpec((n_exp, 1), lambda i: (0, 0)),
                  pl.BlockSpec((tm, tm), lambda i: (0, 0))],
        out_specs=[pl.BlockSpec((TOP_K, tm), lambda i: (0, i)),
                   pl.BlockSpec((TOP_K, tm), lambda i: (0, i)),
                   pl.BlockSpec((TOP_K, tm), lambda i: (0, i)),
                   pl.BlockSpec((n_exp, LANES), lambda i: (0, 0))],
        scratch_shapes=[pltpu.VMEM((n_exp, 1), F32)],
        compiler_params=_cp("arbitrary"),
        name="moe_router",
    )(ub, router_w.T.astype(BF16), router_bias.reshape(n_exp, 1).astype(F32), tri)
    return outs


def _dispatch_kernel(pos_ref, up_ref, xg_ref, sem, *, tm, top_k):
    base = pl.program_id(0) * tm

    def copy(t, k):
        p = pos_ref[(base + t) * top_k + k]
        return pltpu.make_async_copy(up_ref.at[pl.ds(t, 1), :], xg_ref.at[pl.ds(p, 1), :], sem)

    def issue(t, carry):
        for k in range(top_k):
            copy(t, k).start()
        return carry

    def drain(t, carry):
        for k in range(top_k):
            copy(t, k).wait()
        return carry

    lax.fori_loop(0, tm, issue, 0)
    lax.fori_loop(0, tm, drain, 0)


def _dispatch(pos_flat, up, p_rows):
    n, w = up.shape
    tm = _pick(n, ROUTER_TILE)
    return pl.pallas_call(
        functools.partial(_dispatch_kernel, tm=tm, top_k=TOP_K),
        out_shape=jax.ShapeDtypeStruct((p_rows, w), U32),
        grid_spec=pltpu.PrefetchScalarGridSpec(
            num_scalar_prefetch=1,
            grid=(n // tm,),
            in_specs=[pl.BlockSpec((tm, w), lambda i, pos: (i, 0))],
            out_specs=pl.BlockSpec(memory_space=pl.ANY),
            scratch_shapes=[pltpu.SemaphoreType.DMA]),
        compiler_params=_cp("arbitrary"),
        name="moe_dispatch",
    )(pos_flat, up)


def _expert_kernel(te_ref, last_ref, xg_ref, wg_ref, wu_ref, wd_ref, yg_ref, wgu_b, wd_b):
    j = pl.program_id(0)
    valid = j <= last_ref[0]
    ff = wg_ref.shape[-1]

    @pl.when((j == 0) | (te_ref[j] != te_ref[jnp.maximum(j - 1, 0)]))
    def _():
        wgu_b[:, 0:ff] = wg_ref[...].astype(BF16)
        wgu_b[:, ff:2 * ff] = wu_ref[...].astype(BF16)
        wd_b[...] = wd_ref[...].astype(BF16)

    @pl.when(valid)
    def _():
        lo, hi = _load_packed(xg_ref)
        x = jnp.concatenate([lo, hi], axis=1).astype(BF16)
        gu = jnp.dot(x, wgu_b[...], preferred_element_type=F32)
        g, u = gu[:, 0:ff], gu[:, ff:2 * ff]
        h = (g * jax.nn.sigmoid(g) * u).astype(BF16)
        _store_packed(yg_ref, jnp.dot(h, wd_b[...], preferred_element_type=F32))


def _experts(tile_e, last_tile, xg, wg, wu, wd, layer, te_rows):
    n_tiles = tile_e.shape[0]
    _, _, d, ff = wg.shape
    w = xg.shape[1]
    return pl.pallas_call(
        _expert_kernel,
        out_shape=jax.ShapeDtypeStruct(xg.shape, U32),
        grid_spec=pltpu.PrefetchScalarGridSpec(
            num_scalar_prefetch=2,
            grid=(n_tiles,),
            in_specs=[pl.BlockSpec((te_rows, w), lambda j, te, lt: (jnp.minimum(j, lt[0]), 0)),
                      pl.BlockSpec((None, None, d, ff), lambda j, te, lt: (layer, te[j], 0, 0)),
                      pl.BlockSpec((None, None, d, ff), lambda j, te, lt: (layer, te[j], 0, 0)),
                      pl.BlockSpec((None, None, ff, d), lambda j, te, lt: (layer, te[j], 0, 0))],
            out_specs=pl.BlockSpec((te_rows, w), lambda j, te, lt: (jnp.minimum(j, lt[0]), 0)),
            scratch_shapes=[pltpu.VMEM((d, 2 * ff), BF16), pltpu.VMEM((ff, d), BF16)]),
        compiler_params=_cp("arbitrary"),
        name="moe_experts",
    )(tile_e, last_tile, xg, wg, wu, wd)


def _shared_kernel(u_ref, sg_ref, su_ref, sd_ref, o_ref):
    x = u_ref[...]
    g = jnp.dot(x, sg_ref[...], preferred_element_type=F32)
    u = jnp.dot(x, su_ref[...], preferred_element_type=F32)
    h = (g * jax.nn.sigmoid(g) * u).astype(BF16)
    o_ref[...] = jnp.dot(h, sd_ref[...], preferred_element_type=F32).astype(o_ref.dtype)


def _shared_expert(ub, sg, su, sd):
    n, d = ub.shape
    ff = sg.shape[1]
    tm = _pick(n, 512)
    return pl.pallas_call(
        _shared_kernel,
        out_shape=jax.ShapeDtypeStruct((n, d), BF16),
        grid=(n // tm,),
        in_specs=[pl.BlockSpec((tm, d), lambda i: (i, 0)),
                  pl.BlockSpec((d, ff), lambda i: (0, 0)),
                  pl.BlockSpec((d, ff), lambda i: (0, 0)),
                  pl.BlockSpec((ff, d), lambda i: (0, 0))],
        out_specs=pl.BlockSpec((tm, d), lambda i: (i, 0)),
        compiler_params=_cp("parallel"),
        name="moe_shared",
    )(ub, sg, su, sd)


def _combine_kernel(pos_ref, yg_ref, gw_ref, ysh_ref, x_ref, gate_ref, lng_ref, lnb_ref,
                    sc_ref, sh_ref, *rest, tc, top_k, alpha, kinds):
    outs, (gbuf, ybuf, sem) = rest[:len(kinds)], rest[len(kinds):]
    base = pl.program_id(0) * tc
    half = gbuf.shape[-1]

    def copy(t, k):
        p = pos_ref[(base + t) * top_k + k]
        return pltpu.make_async_copy(yg_ref.at[pl.ds(p, 1), :], gbuf.at[k, pl.ds(t, 1), :], sem)

    def issue(t, carry):
        for k in range(top_k):
            copy(t, k).start()
        return carry

    def drain(t, carry):
        for k in range(top_k):
            copy(t, k).wait()
        return carry

    lax.fori_loop(0, tc, issue, 0)
    lax.fori_loop(0, tc, drain, 0)

    rc, cw = _pick(tc, 64), _pick(half, 256)
    for r in range(tc // rc):
        rows = slice(r * rc, (r + 1) * rc)
        gw = gw_ref[rows, :]
        for c in range(half // cw):
            cols = slice(c * cw, (c + 1) * cw)
            acc_lo = acc_hi = None
            for k in range(top_k):
                lo, hi = _load_packed(gbuf.at[k, rows, cols])
                wk = gw[:, k:k + 1]
                acc_lo = lo * wk if k == 0 else acc_lo + lo * wk
                acc_hi = hi * wk if k == 0 else acc_hi + hi * wk
            ybuf[rows, cols] = acc_lo
            ybuf[rows, half + c * cw:half + (c + 1) * cw] = acc_hi
    y = ybuf[...] + ysh_ref[...].astype(F32)
    z = alpha * x_ref[...].astype(F32) + (1.0 + gate_ref[...]) * y
    xn = _layer_norm_rows(z, lng_ref[...], lnb_ref[...])
    _emit_outputs(xn, sc_ref, sh_ref, outs, kinds)


def _combine(pos_flat, yg, gate_t, ysh, x2, mod_l, gate_chunk, lng, lnb, next_mod, sc_chunk,
             sh_chunk, seq, alpha, kinds):
    n, d = x2.shape
    tc = _pick(seq, COMBINE_TILE)
    tpb = seq // tc
    shapes, specs = _out_shapes_specs(kinds, n, d, tc)

    def mspec(chunk):
        return pl.BlockSpec((None, 1, d), lambda i, pos, _c=chunk: (i // tpb, 0, _c))

    return pl.pallas_call(
        functools.partial(_combine_kernel, tc=tc, top_k=TOP_K, alpha=alpha, kinds=kinds),
        out_shape=shapes,
        grid_spec=pltpu.PrefetchScalarGridSpec(
            num_scalar_prefetch=1,
            grid=(n // tc,),
            in_specs=[pl.BlockSpec(memory_space=pl.ANY),
                      pl.BlockSpec((tc, TOP_K), lambda i, pos: (i, 0)),
                      pl.BlockSpec((tc, d), lambda i, pos: (i, 0)),
                      pl.BlockSpec((tc, d), lambda i, pos: (i, 0)),
                      mspec(gate_chunk),
                      pl.BlockSpec((1, d), lambda i, pos: (0, 0)),
                      pl.BlockSpec((1, d), lambda i, pos: (0, 0)),
                      mspec(sc_chunk), mspec(sh_chunk)],
            out_specs=specs,
            scratch_shapes=[pltpu.VMEM((TOP_K, tc, d // 2), U32), pltpu.VMEM((tc, d), F32),
                            pltpu.SemaphoreType.DMA]),
        compiler_params=_cp("arbitrary"),
        name="moe_combine",
    )(pos_flat, yg, gate_t, ysh, x2, mod_l, lng.reshape(1, d), lnb.reshape(1, d),
      next_mod, next_mod)


def _moe_sublayer(x2, ub, up, mod_l, lng, lnb, next_mod, next_chunks, seq, alpha, kinds,
                  router_w, router_bias, wg, wu, wd, layer, sg, su, sd):
    n, d = x2.shape
    n_exp = router_w.shape[1]
    te_rows = _pick(n * TOP_K // n_exp, EXPERT_TILE)
    p_rows = n * TOP_K + n_exp * te_rows

    eid, gate, rk, cnt = _router(ub, router_w, router_bias)
    counts = cnt[:, 0].astype(I32)
    padded = (counts + te_rows - 1) // te_rows * te_rows
    ends = jnp.cumsum(padded)
    offs = ends - padded
    onehot = eid[:, :, None] == jnp.arange(n_exp, dtype=I32)
    pos_flat = (jnp.sum(jnp.where(onehot, offs, 0), axis=-1) + rk).T.reshape(-1)
    tile_start = jnp.arange(p_rows // te_rows, dtype=I32) * te_rows
    tile_e = jnp.minimum(jnp.sum((ends[None, :] <= tile_start[:, None]).astype(I32), axis=1),
                         n_exp - 1)
    last_tile = (ends[-1:] // te_rows - 1).astype(I32)

    xg = _dispatch(pos_flat, up, p_rows)
    yg = _experts(tile_e, last_tile, xg, wg, wu, wd, layer, te_rows)
    ysh = _shared_expert(ub, sg, su, sd)
    return _combine(pos_flat, yg, gate.T, ysh, x2, mod_l, 5, lng, lnb, next_mod,
                    next_chunks[0], next_chunks[1], seq, alpha, kinds)


def _attn_kernel(qi_ref, ki_ref, slope_ref, q_ref, k_ref, v_ref, lamp_ref, g_ref, o_ref,
                 m_ref, l_ref, acc_ref, s_ref, p_ref, *, tq, tk, rq, ck, lam_init):
    step = pl.program_id(2)
    qi = qi_ref[step]
    ki = ki_ref[step]
    slope = slope_ref[pl.program_id(1)]

    @pl.when(ki == 0)
    def _():
        m_ref[...] = jnp.full_like(m_ref, -jnp.inf)
        l_ref[...] = jnp.zeros_like(l_ref)
        acc_ref[...] = jnp.zeros_like(acc_ref)

    col = lax.broadcasted_iota(I32, (1, tk), 1)
    colbias = (slope * LOG2E) * (ki * tk - qi * tq + col).astype(F32)

    def fold(x, op):
        out = x[:, 0:LANES]
        for j in range(1, x.shape[1] // LANES):
            out = op(out, x[:, j * LANES:(j + 1) * LANES])
        return out

    def update(masked):
        for rb in range(tq // rq):
            rows = slice(rb * rq, (rb + 1) * rq)
            n_ck = ((rb + 1) * rq + ck - 1) // ck if masked else tk // ck
            kw = n_ck * ck
            for br in range(2):
                cols = slice(br * HEAD_DIM, (br + 1) * HEAD_DIM)
                q_blk = q_ref[rows, cols]
                mrun = None
                for cc in range(n_ck):
                    kcols = slice(cc * ck, (cc + 1) * ck)
                    s = lax.dot_general(q_blk, k_ref[kcols, cols], (((1,), (1,)), ((), ())),
                                        preferred_element_type=F32) + colbias[:, kcols]
                    if masked and (cc + 1) * ck > rb * rq + 1:
                        keep = (lax.broadcasted_iota(I32, (rq, ck), 0) + rb * rq
                                >= lax.broadcasted_iota(I32, (rq, ck), 1) + cc * ck)
                        s = jnp.where(keep, s, -jnp.inf)
                    s_ref[:, kcols] = s
                    part = fold(s, jnp.maximum)
                    mrun = part if mrun is None else jnp.maximum(mrun, part)
                m_prev = m_ref[br, rows]
                m_new = jnp.maximum(m_prev, jnp.max(mrun, axis=-1, keepdims=True))
                a = jnp.exp2(m_prev - m_new)
                lrun = None
                for cc in range(n_ck):
                    kcols = slice(cc * ck, (cc + 1) * ck)
                    p = jnp.exp2(s_ref[:, kcols] - m_new)
                    p_ref[:, kcols] = p.astype(BF16)
                    part = fold(p, jnp.add)
                    lrun = part if lrun is None else lrun + part
                l_ref[br, rows] = a * l_ref[br, rows] + jnp.sum(lrun, axis=-1, keepdims=True)
                acc_ref[br, rows] = a * acc_ref[br, rows] + jnp.dot(
                    p_ref[:, 0:kw], v_ref[0:kw, :], preferred_element_type=F32)
                m_ref[br, rows] = m_new

    @pl.when(ki < qi)
    def _():
        update(False)

    @pl.when(ki == qi)
    def _():
        update(True)
        lp = lamp_ref[...]
        lam = (jnp.exp(jnp.sum(lp[0:1] * lp[1:2], axis=-1, keepdims=True))
               - jnp.exp(jnp.sum(lp[2:3] * lp[3:4], axis=-1, keepdims=True)) + lam_init)
        o = acc_ref[0] / l_ref[0] - lam * (acc_ref[1] / l_ref[1])
        o = o * lax.rsqrt(jnp.mean(o * o, axis=-1, keepdims=True) + LN_EPS) * g_ref[...]
        o_ref[...] = (o * (1.0 - lam_init)).astype(o_ref.dtype)


def _diff_attention(q, k, v, lam_p, subln_g, batch, seq, lam_init):
    n, d = q.shape
    dv = 2 * HEAD_DIM
    n_heads = d // dv
    t = _pick(seq, ATTN_TILE)
    nq = seq // t
    pairs = [(a, b) for a in range(nq) for b in range(a + 1)]
    qi = jnp.asarray([p[0] for p in pairs], I32)
    ki = jnp.asarray([p[1] for p in pairs], I32)
    slopes = 2.0 ** (-8.0 * jnp.arange(1, n_heads + 1, dtype=F32) / n_heads)
    q3, k3, v3 = (a.reshape(batch, seq, d) for a in (q, k, v))
    rq, ck = _pick(t, ATTN_ROWS), _pick(t, ATTN_COLS)
    out = pl.pallas_call(
        functools.partial(_attn_kernel, tq=t, tk=t, rq=rq, ck=ck, lam_init=lam_init),
        out_shape=jax.ShapeDtypeStruct((batch, seq, d), BF16),
        grid_spec=pltpu.PrefetchScalarGridSpec(
            num_scalar_prefetch=3,
            grid=(batch, n_heads, len(pairs)),
            in_specs=[pl.BlockSpec((None, t, dv), lambda b, h, s, qi, ki, sl: (b, qi[s], h)),
                      pl.BlockSpec((None, t, dv), lambda b, h, s, qi, ki, sl: (b, ki[s], h)),
                      pl.BlockSpec((None, t, dv), lambda b, h, s, qi, ki, sl: (b, ki[s], h)),
                      pl.BlockSpec((4, HEAD_DIM), lambda b, h, s, qi, ki, sl: (0, 0)),
                      pl.BlockSpec((1, dv), lambda b, h, s, qi, ki, sl: (0, 0))],
            out_specs=pl.BlockSpec((None, t, dv), lambda b, h, s, qi, ki, sl: (b, qi[s], h)),
            scratch_shapes=[pltpu.VMEM((2, t, 1), F32), pltpu.VMEM((2, t, 1), F32),
                            pltpu.VMEM((2, t, dv), F32),
                            pltpu.VMEM((rq, t), F32), pltpu.VMEM((rq, t), BF16)]),
        compiler_params=_cp("parallel", "parallel", "arbitrary"),
        name="diff_attention",
    )(qi, ki, slopes, q3, k3, v3, lam_p.astype(F32), subln_g.reshape(1, dv).astype(F32))
    return out.reshape(n, d)


def kernel(x, c, ada_w, ada_b, ln_g, ln_b, conv_pw1_w, conv_pw1_b, conv_dw_w, conv_dw_b,
           conv_ln_g, conv_ln_b, conv_pw2_w, conv_pw2_b, attn_wk, attn_wv, attn_wq,
           attn_lambda, attn_subln_g, attn_wo, router_w, router_bias, exp_w_gate, exp_w_up,
           exp_w_down, shared_w_gate, shared_w_up, shared_w_down):
    batch, seq, d = x.shape
    depth = ada_w.shape[0]
    n_a = conv_pw1_w.shape[0]
    n = batch * seq
    alpha = (2 * depth) ** 0.25
    zeros_d = jnp.zeros((d,), F32)

    mod = _ada_mod(c, ada_w, ada_b)
    mods = [mod[l].reshape(batch, 1, 6 * d) for l in range(depth)]
    x2 = x.reshape(n, d)
    xb = None
    ub = _modulate(x2, mods[0], seq, 1, 0)
    kb = vb = None

    for l in range(depth):
        mod_l = mods[l]
        if l < n_a:
            h = _matmul_glu(ub, conv_pw1_w[l].astype(BF16), conv_pw1_b[l])
            h = _conv_ln_swish(h, conv_dw_w[l], conv_dw_b[l], conv_ln_g[l], conv_ln_b[l],
                               batch, seq)
            y = _matmul(h, conv_pw2_w[l].astype(BF16), conv_pw2_b[l])
        else:
            j = l - n_a
            if l == n_a:
                kb = _matmul(xb, attn_wk.astype(BF16), zeros_d)
                vb = _matmul(xb, attn_wv.astype(BF16), zeros_d)
            qb = _matmul(ub, attn_wq[j].astype(BF16), zeros_d, scale=HEAD_DIM ** -0.5 * LOG2E)
            lam_init = 0.8 - 0.6 * math.exp(-0.3 * l)
            o = _diff_attention(qb, kb, vb, attn_lambda[j], attn_subln_g[j], batch, seq, lam_init)
            y = _matmul(o, attn_wo[j].astype(BF16), zeros_d)
        x2, ub, up = _post(x2, y, mod_l, 2, ln_g[l, 0], ln_b[l, 0], mod_l, 4, 3, seq, alpha,
                           ("xb", "ub", "up"))
        last = l == depth - 1
        kinds = ("x",) if last else ("xb", "ub")
        next_mod = mod_l if last else mods[l + 1]
        outs = _moe_sublayer(
            x2, ub, up, mod_l, ln_g[l, 1], ln_b[l, 1], next_mod, (1, 0), seq, alpha, kinds,
            router_w[l], router_bias[l], exp_w_gate, exp_w_up, exp_w_down, l,
            shared_w_gate[l].astype(BF16),
            shared_w_up[l].astype(BF16), shared_w_down[l].astype(BF16))
        x2 = xb = outs[0]
        if not last:
            ub = outs[-1]
    return x2.reshape(batch, seq, d)
```
